```python
import math
import jax, jax.numpy as jnp
from jax import lax
import numpy as np

D_MODEL = 2048
BATCH = 8
SEQ = 2048
DEPTH = 2

GRID_W = 64
CTX_LEN = 256
N_GROUPS = 4
GROUP_W = D_MODEL // N_GROUPS
MIX_W = N_GROUPS * GROUP_W
HEAD_DIM = 64
CONV_A_WIDTH = 31
DIFF_HEADS = GROUP_W // (2 * HEAD_DIM)
ROPE_BASE = 10000.0
Q_BLOCK = 128
CONV_C_WIDTH = 3
LRU_BLOCKS = GROUP_W // HEAD_DIM
LRU_CONV_WIDTH = 4
LRU_C = 8.0
N_EXPERTS = 32
TOP_K = 4
D_FF = D_MODEL // 2
SWIGLU_LIMIT = 7.0
SWIGLU_ALPHA = 1.702
EPS = 1e-6
COLS_A = 2 * GROUP_W
COLS_B = 3 * GROUP_W
COLS_C = 3 * GROUP_W
COLS_D = 2 * GROUP_W
N_IN = COLS_A + COLS_B + COLS_C + COLS_D
COL_SPLITS = (COLS_A, COLS_A + COLS_B, COLS_A + COLS_B + COLS_C)
F32 = jnp.float32

kernel_name = 'hybrid_parallel_groups_moe_dit_block'


def rms_norm(x, g):
    xf = x.astype(F32)
    y = xf * lax.rsqrt(jnp.mean(xf * xf, axis=-1, keepdims=True) + EPS)
    return (y * g).astype(x.dtype)


def layer_norm(x, g, b):
    xf = x.astype(F32)
    mu = jnp.mean(xf, axis=-1, keepdims=True)
    var = jnp.mean(jnp.square(xf - mu), axis=-1, keepdims=True)
    return ((xf - mu) * lax.rsqrt(var + EPS) * g + b).astype(x.dtype)


def depthwise_conv(x, w, pad):
    return lax.conv_general_dilated(
        x, w[:, None, :].astype(x.dtype), (1,), [pad],
        dimension_numbers=('NWC', 'WIO', 'NWC'), feature_group_count=x.shape[-1])


def axial_rope_tables(seq_len):
    rows = seq_len // GRID_W
    row = jnp.repeat(jnp.arange(rows, dtype=F32), GRID_W)
    col = jnp.tile(jnp.arange(GRID_W, dtype=F32), rows)
    half = HEAD_DIM // 2
    inv = ROPE_BASE ** (-jnp.arange(0, half, 2, dtype=F32) / half)
    ar = row[:, None] * inv
    ac = col[:, None] * inv
    ang = jnp.concatenate([ar, ar, ac, ac], axis=-1)
    return jnp.cos(ang), jnp.sin(ang)


def apply_axial_rope(x, cos, sin):
    xf = x.astype(F32)
    x1, x2, x3, x4 = jnp.split(xf, 4, axis=-1)
    rot = jnp.concatenate([-x2, x1, -x4, x3], axis=-1)
    return (xf * cos[:, None, :] + rot * sin[:, None, :]).astype(x.dtype)


def conformer_conv(u, conv_w, conv_b, ln_g, ln_b):
    val, gate = jnp.split(u, 2, axis=-1)
    z = val * jax.nn.sigmoid(gate)
    half = (CONV_A_WIDTH - 1) // 2
    z = depthwise_conv(z, conv_w, (half, half)) + conv_b
    return jax.nn.silu(layer_norm(z, ln_g, ln_b))


def split_diff_heads(u):
    bsz, n, _ = u.shape
    q, k, v = jnp.split(u, 3, axis=-1)
    return (q.reshape(bsz, n, 2 * DIFF_HEADS, HEAD_DIM),
            k.reshape(bsz, n, 2 * DIFF_HEADS, HEAD_DIM),
            v.reshape(bsz, n, DIFF_HEADS, 2 * HEAD_DIM))


def diff_softmax_attend(q, k, v, lam):
    s = jnp.einsum('bqhd,bkhd->bhqk', q, k).astype(F32) * (HEAD_DIM ** -0.5)
    p = jax.nn.softmax(s, axis=-1)
    bsz, _, lq, lk = p.shape
    p = p.reshape(bsz, DIFF_HEADS, 2, lq, lk)
    w = (p[:, :, 0] - lam * p[:, :, 1]).astype(v.dtype)
    return jnp.einsum('bhqk,bkhe->bqhe', w, v)


def diff_head_norm(o, g, lam_init):
    o = rms_norm(o, g) * (1.0 - lam_init)
    return o.reshape(o.shape[0], o.shape[1], DIFF_HEADS * 2 * HEAD_DIM)


def diff_attention(ub, cb, lam_vecs, norm_g, lam_init, cos, sin, need_ctx):
    q, k, v = split_diff_heads(ub)
    qc, kc, vc = split_diff_heads(cb)
    q = apply_axial_rope(q, cos, sin)
    k = apply_axial_rope(k, cos, sin)
    lv = lam_vecs.astype(F32)
    lam = jnp.exp(jnp.sum(lv[0] * lv[1])) - jnp.exp(jnp.sum(lv[2] * lv[3])) + lam_init
    k_all = jnp.concatenate([kc, k], axis=1)
    v_all = jnp.concatenate([vc, v], axis=1)
    bsz, seq_len = q.shape[0], q.shape[1]
    n_blk = seq_len // Q_BLOCK
    q_blocks = q.reshape(bsz, n_blk, Q_BLOCK, 2 * DIFF_HEADS, HEAD_DIM).swapaxes(0, 1)
    o = lax.map(lambda qb: diff_softmax_attend(qb, k_all, v_all, lam), q_blocks)
    o = o.swapaxes(0, 1).reshape(bsz, seq_len, DIFF_HEADS, 2 * HEAD_DIM)
    y = diff_head_norm(o, norm_g, lam_init)
    y_ctx = diff_head_norm(diff_softmax_attend(qc, kc, vc, lam), norm_g, lam_init) if need_ctx else None
    return y, y_ctx


def gated_short_conv(u, conv_w, norm_g):
    bg, cg, v = jnp.split(u, 3, axis=-1)
    half = (CONV_C_WIDTH - 1) // 2
    y = bg * depthwise_conv(cg * v, conv_w, (half, half))
    return rms_norm(y, norm_g)


def linear_scan(a, b, h0, reverse):
    idx = -1 if reverse else 0
    b = b.at[:, idx].add(a[:, idx] * h0)

    def combine(left, right):
        a_l, b_l = left
        a_r, b_r = right
        return a_l * a_r, a_r * b_l + b_r

    _, h = lax.associative_scan(combine, (a, b), reverse=reverse, axis=1)
    return h


def rglru_direction(xr, conv_w, conv_b, wa, ba, wx, bx, lam, h0, reverse):
    pad = (0, LRU_CONV_WIDTH - 1) if reverse else (LRU_CONV_WIDTH - 1, 0)
    xcv = depthwise_conv(xr, conv_w, pad) + conv_b
    bsz, n, w = xcv.shape
    xh = xcv.reshape(bsz, n, LRU_BLOCKS, w // LRU_BLOCKS)
    r = jax.nn.sigmoid(jnp.einsum('blhi,hij->blhj', xh, wa).reshape(bsz, n, w) + ba)
    ig = jax.nn.sigmoid(jnp.einsum('blhi,hij->blhj', xh, wx).reshape(bsz, n, w) + bx)
    log_a = -LRU_C * r.astype(F32) * jax.nn.softplus(-lam.astype(F32))
    a = jnp.exp(log_a)
    b = jnp.sqrt(-jnp.expm1(2.0 * log_a)) * (ig * xcv).astype(F32)
    h = linear_scan(a, b, h0, reverse)
    h_last = h[:, 0] if reverse else h[:, -1]
    return h, h_last


def bidir_rglru(ud, cd, conv_w, conv_b, wa, ba, wx, bx, lam, norm_g, need_ctx):
    gate_l, rec_l = jnp.split(ud, 2, axis=-1)
    gate_c, rec_c = jnp.split(cd, 2, axis=-1)
    h0 = jnp.zeros((cd.shape[0], GROUP_W), F32)
    out_l = jnp.zeros(rec_l.shape, F32)
    out_c = jnp.zeros(rec_c.shape, F32)
    for d in range(2):
        rev = d == 1
        prm = (conv_w[d], conv_b[d], wa[d], ba[d], wx[d], bx[d], lam[d])
        hs_c, h_ctx_final = rglru_direction(rec_c, *prm, h0, rev)
        hs_l, _ = rglru_direction(rec_l, *prm, h_ctx_final, rev)
        out_l = out_l + hs_l
        out_c = out_c + hs_c
    y = rms_norm(jax.nn.gelu(gate_l) * out_l.astype(ud.dtype), norm_g)
    y_ctx = rms_norm(jax.nn.gelu(gate_c) * out_c.astype(cd.dtype), norm_g) if need_ctx else None
    return y, y_ctx


def token_mixers(h, hc, w_in, w_out, conv_a_w, conv_a_b, ln_a_g, ln_a_b, diff_lambda, diff_norm_g,
                 conv_c_w, norm_c_g, lru_conv_w, lru_conv_b, lru_wa, lru_ba, lru_wx, lru_bx,
                 lru_lam, norm_d_g, cos, sin, lam_init, need_ctx):
    ua, ub, us, ud = jnp.split(h @ w_in, COL_SPLITS, axis=-1)
    ca, cb, cs, cd = jnp.split(hc @ w_in, COL_SPLITS, axis=-1)
    ya = conformer_conv(ua, conv_a_w, conv_a_b, ln_a_g, ln_a_b)
    yb, yb_c = diff_attention(ub, cb, diff_lambda, diff_norm_g, lam_init, cos, sin, need_ctx)
    ys = gated_short_conv(us, conv_c_w, norm_c_g)
    yd, yd_c = bidir_rglru(ud, cd, lru_conv_w, lru_conv_b, lru_wa, lru_ba, lru_wx, lru_bx,
                           lru_lam, norm_d_g, need_ctx)
    y = jnp.concatenate([ya, yb, ys, yd], axis=-1) @ w_out
    y_ctx = None
    if need_ctx:
        ya_c = conformer_conv(ca, conv_a_w, conv_a_b, ln_a_g, ln_a_b)
        ys_c = gated_short_conv(cs, conv_c_w, norm_c_g)
        y_ctx = jnp.concatenate([ya_c, yb_c, ys_c, yd_c], axis=-1) @ w_out
    return y, y_ctx


def moe_ffn(t, router_w, router_b, w1, b1, w2, b2):
    logits = (t @ router_w + router_b).astype(F32)
    top_v, top_i = lax.top_k(logits, TOP_K)
    gates = jax.nn.softmax(top_v, axis=-1)
    combine = jnp.einsum('tk,tke->te', gates, jax.nn.one_hot(top_i, N_EXPERTS, dtype=F32))
    acc = jnp.zeros(t.shape, F32)
    for e in range(N_EXPERTS):
        gu = t @ w1[e] + b1[e]
        g = jnp.minimum(gu[:, 0::2], SWIGLU_LIMIT)
        up = jnp.clip(gu[:, 1::2], -SWIGLU_LIMIT, SWIGLU_LIMIT)
        act = (up + 1.0) * (g * jax.nn.sigmoid(SWIGLU_ALPHA * g))
        acc = acc + combine[:, e:e + 1] * (act @ w2[e] + b2[e]).astype(F32)
    return acc.astype(t.dtype)


def setup_inputs(seed: int = 0) -> dict:
    key = jax.random.key(seed)
    ks = iter(jax.random.split(key, 48))

    def nrm(shape, s):
        return jax.random.normal(next(ks), shape, F32) * s

    L = DEPTH
    u = jax.random.uniform(next(ks), (L, 2, GROUP_W), F32, 0.9, 0.999)
    a = u ** (1.0 / LRU_C)
    lru_lam = jnp.log(a) - jnp.log1p(-a)
    return {
        'x': nrm((BATCH, SEQ, D_MODEL), 1.0),
        'c': nrm((BATCH, D_MODEL), 1.0),
        'ctx': nrm((BATCH, CTX_LEN, D_MODEL), 1.0),
        'c_ctx': nrm((D_MODEL,), 1.0),
        'ada_w': nrm((L, D_MODEL, 6 * D_MODEL), 0.5 * D_MODEL ** -0.5),
        'ada_b': nrm((L, 6 * D_MODEL), 0.02),
        'norm1_g': 1.0 + nrm((L, D_MODEL), 0.02),
        'norm2_g': 1.0 + nrm((L, D_MODEL), 0.02),
        'w_in': nrm((L, D_MODEL, N_IN), D_MODEL ** -0.5),
        'w_out': nrm((L, MIX_W, D_MODEL), MIX_W ** -0.5),
        'conv_a_w': nrm((L, CONV_A_WIDTH, GROUP_W), CONV_A_WIDTH ** -0.5),
        'conv_a_b': nrm((L, GROUP_W), 0.02),
        'ln_a_g': 1.0 + nrm((L, GROUP_W), 0.02),
        'ln_a_b': nrm((L, GROUP_W), 0.02),
        'diff_lambda': nrm((L, 4, HEAD_DIM), 0.1),
        'diff_norm_g': 1.0 + nrm((L, 2 * HEAD_DIM), 0.02),
        'conv_c_w': nrm((L, CONV_C_WIDTH, GROUP_W), CONV_C_WIDTH ** -0.5),
        'norm_c_g': 1.0 + nrm((L, GROUP_W), 0.02),
        'lru_conv_w': nrm((L, 2, LRU_CONV_WIDTH, GROUP_W), LRU_CONV_WIDTH ** -0.5),
        'lru_conv_b': nrm((L, 2, GROUP_W), 0.02),
        'lru_wa': nrm((L, 2, LRU_BLOCKS, HEAD_DIM, HEAD_DIM), HEAD_DIM ** -0.5),
        'lru_ba': nrm((L, 2, GROUP_W), 0.02),
        'lru_wx': nrm((L, 2, LRU_BLOCKS, HEAD_DIM, HEAD_DIM), HEAD_DIM ** -0.5),
        'lru_bx': nrm((L, 2, GROUP_W), 0.02),
        'lru_lam': lru_lam,
        'norm_d_g': 1.0 + nrm((L, GROUP_W), 0.02),
        'router_w': nrm((L, D_MODEL, N_EXPERTS), D_MODEL ** -0.5),
        'router_b': nrm((L, N_EXPERTS), 0.01),
        'exp_w1': nrm((L, N_EXPERTS, D_MODEL, 2 * D_FF), D_MODEL ** -0.5),
        'exp_b1': nrm((L, N_EXPERTS, 2 * D_FF), 0.02),
        'exp_w2': nrm((L, N_EXPERTS, D_FF, D_MODEL), D_FF ** -0.5),
        'exp_b2': nrm((L, N_EXPERTS, D_MODEL), 0.02),
        'final_g': 1.0 + nrm((D_MODEL,), 0.02),
    }


def reference(x, c, ctx, c_ctx, ada_w, ada_b, norm1_g, norm2_g, w_in, w_out,
              conv_a_w, conv_a_b, ln_a_g, ln_a_b, diff_lambda, diff_norm_g,
              conv_c_w, norm_c_g, lru_conv_w, lru_conv_b, lru_wa, lru_ba, lru_wx,
              lru_bx, lru_lam, norm_d_g, router_w, router_b, exp_w1, exp_b1,
              exp_w2, exp_b2, final_g):
    bsz, seq_len, _ = x.shape
    cos, sin = axial_rope_tables(seq_len)
    xc = ctx
    for i in range(DEPTH):
        need_ctx = i < DEPTH - 1
        lam_init = 0.8 - 0.6 * math.exp(-0.3 * i)
        mod = jax.nn.silu(c) @ ada_w[i] + ada_b[i]
        mod_c = jax.nn.silu(c_ctx) @ ada_w[i] + ada_b[i]
        sh1, sc1, g1, sh2, sc2, g2 = jnp.split(mod[:, None, :], 6, axis=-1)
        csh1, csc1, cg1, csh2, csc2, cg2 = jnp.split(mod_c, 6, axis=-1)
        h = rms_norm(x, norm1_g[i]) * (1.0 + sc1) + sh1
        hc = rms_norm(xc, norm1_g[i]) * (1.0 + csc1) + csh1
        y, y_ctx = token_mixers(h, hc, w_in[i], w_out[i], conv_a_w[i], conv_a_b[i], ln_a_g[i], ln_a_b[i],
                                diff_lambda[i], diff_norm_g[i], conv_c_w[i], norm_c_g[i],
                                lru_conv_w[i], lru_conv_b[i], lru_wa[i], lru_ba[i], lru_wx[i], lru_bx[i],
                                lru_lam[i], norm_d_g[i], cos, sin, lam_init, need_ctx)
        x = x + g1 * y
        h2 = rms_norm(x, norm2_g[i]) * (1.0 + sc2) + sh2
        moe_args = (router_w[i], router_b[i], exp_w1[i], exp_b1[i], exp_w2[i], exp_b2[i])
        if need_ctx:
            xc = xc + cg1 * y_ctx
            hc2 = rms_norm(xc, norm2_g[i]) * (1.0 + csc2) + csh2
            n_lat = bsz * seq_len
            tokens = jnp.concatenate([h2.reshape(n_lat, D_MODEL), hc2.reshape(-1, D_MODEL)], axis=0)
            f = moe_ffn(tokens, *moe_args)
            x = x + g2 * f[:n_lat].reshape(x.shape)
            xc = xc + cg2 * f[n_lat:].reshape(xc.shape)
        else:
            x = x + g2 * moe_ffn(h2.reshape(-1, D_MODEL), *moe_args).reshape(x.shape)
    return rms_norm(x, final_g)
```

```python
import functools
import math

import jax
import jax.numpy as jnp
from jax import lax
from jax.experimental import pallas as pl
from jax.experimental.pallas import tpu as pltpu

F32 = jnp.float32
BF16 = jnp.bfloat16
I32 = jnp.int32
U32 = jnp.uint32

GRID_W = 64
GROUP_W = 512
HEAD_DIM = 64
DIFF_HEADS = 4
CONV_A_WIDTH = 31
CONV_C_WIDTH = 3
LRU_CONV_WIDTH = 4
LRU_C = 8.0
ROPE_BASE = 10000.0
N_EXPERTS = 32
TOP_K = 4
SWIGLU_LIMIT = 7.0
SWIGLU_ALPHA = 1.702
EPS = 1e-6

LANES = 128
HALO = 16
MOD_ROWS = 16
VMEM_LIMIT = 56 * 1024 * 1024

COL_VAL, COL_GATE, COL_Q, COL_K, COL_V, COL_BG, COL_CG, COL_VC, COL_GD, COL_REC = range(10)


def _params(sem, vmem=VMEM_LIMIT):
    return pltpu.CompilerParams(dimension_semantics=sem, vmem_limit_bytes=vmem)


def _sigmoid(x):
    return 1.0 / (1.0 + jnp.exp(-x))


def _mod_body(c_ref, w_ref, b_ref, o_ref):
    c = c_ref[...]
    s = (c * _sigmoid(c)).astype(BF16)
    o_ref[...] = jnp.dot(s, w_ref[...].astype(BF16), preferred_element_type=F32) + b_ref[...]


def _modulation(cc, ada_w, ada_b):
    depth, d, n = ada_w.shape
    tn = 1024
    return pl.pallas_call(
        _mod_body,
        grid=(depth, n // tn),
        in_specs=[
            pl.BlockSpec((MOD_ROWS, d), lambda l, j: (0, 0)),
            pl.BlockSpec((None, d, tn), lambda l, j: (l, 0, j)),
            pl.BlockSpec((None, 1, tn), lambda l, j: (l, 0, j)),
        ],
        out_specs=pl.BlockSpec((None, MOD_ROWS, tn), lambda l, j: (l, 0, j)),
        out_shape=jax.ShapeDtypeStruct((depth, MOD_ROWS, n), F32),
        compiler_params=_params(("arbitrary", "arbitrary")),
        name="modulation",
    )(cc, ada_w, ada_b.reshape(depth, 1, n))


def _inproj_body(x_ref, g_ref, sc_ref, sh_ref, w_ref, o_ref, h_scr, *, slab):
    j = pl.program_id(1)
    tm = x_ref.shape[0]

    @pl.when(j == 0)
    def _():
        g = g_ref[...]
        sc = 1.0 + sc_ref[0]
        sh = sh_ref[0]

        def rows(s, _):
            r0 = pl.multiple_of(s * slab, slab)
            x = x_ref[pl.ds(r0, slab), :]
            inv = lax.rsqrt(jnp.mean(x * x, axis=-1, keepdims=True) + EPS)
            h_scr[pl.ds(r0, slab), :] = ((x * inv * g) * sc + sh).astype(BF16)
            return 0

        lax.fori_loop(0, tm // slab, rows, 0)

    o_ref[...] = jnp.dot(h_scr[...], w_ref[...], preferred_element_type=F32).astype(o_ref.dtype)


def _inproj(x, mod3, norm_g, w_in, *, n_lat_rows, seq, bsz, tm):
    t_all, d = x.shape
    n_in = w_in.shape[1]
    tn = 1280
    nlt = n_lat_rows // tm
    per_seq = seq // tm

    def mrow(i):
        return jnp.where(i < nlt, i // per_seq, bsz)

    return pl.pallas_call(
        functools.partial(_inproj_body, slab=min(128, tm)),
        grid=(t_all // tm, n_in // tn),
        in_specs=[
            pl.BlockSpec((tm, d), lambda i, j: (i, 0)),
            pl.BlockSpec((1, d), lambda i, j: (0, 0)),
            pl.BlockSpec((1, 1, d), lambda i, j: (mrow(i), 0, 1)),
            pl.BlockSpec((1, 1, d), lambda i, j: (mrow(i), 0, 0)),
            pl.BlockSpec((d, tn), lambda i, j: (0, j)),
        ],
        out_specs=pl.BlockSpec((tm, tn), lambda i, j: (i, j)),
        out_shape=jax.ShapeDtypeStruct((t_all, n_in), BF16),
        scratch_shapes=[pltpu.VMEM((tm, d), BF16)],
        compiler_params=_params(("arbitrary", "arbitrary")),
        name="inproj",
    )(x, norm_g.reshape(1, d), mod3, mod3, w_in)


def _convmix_body(val_ref, gate_ref, pval_ref, pgate_ref, nval_ref, ngate_ref,
                  bg_ref, cg_ref, vc_ref, pcg_ref, pvc_ref, ncg_ref, nvc_ref,
                  caw_ref, cab_ref, lng_ref, lnb_ref, ccw_ref, ncg_g_ref,
                  ya_ref, ys_ref, zp_scr, mp_scr, *, nlt, per_lat, per_ctx, chunk):
    i = pl.program_id(0)
    tr = val_ref.shape[0]
    pos = jnp.where(i < nlt, i % per_lat, (i - nlt) % per_ctx)
    per = jnp.where(i < nlt, per_lat, per_ctx)
    keep_prev = (pos > 0).astype(F32)
    keep_next = (pos < per - 1).astype(F32)

    def glu(v_ref, g_ref):
        return v_ref[...].astype(F32) * _sigmoid(g_ref[...].astype(F32))

    zp_scr[0:HALO, :] = glu(pval_ref, pgate_ref) * keep_prev
    zp_scr[HALO:HALO + tr, :] = glu(val_ref, gate_ref)
    zp_scr[HALO + tr:, :] = glu(nval_ref, ngate_ref) * keep_next

    mp_scr[0:HALO, :] = pcg_ref[...].astype(F32) * pvc_ref[...].astype(F32) * keep_prev
    mp_scr[HALO:HALO + tr, :] = cg_ref[...].astype(F32) * vc_ref[...].astype(F32)
    mp_scr[HALO + tr:, :] = ncg_ref[...].astype(F32) * nvc_ref[...].astype(F32) * keep_next

    half_a = (CONV_A_WIDTH - 1) // 2
    half_c = (CONV_C_WIDTH - 1) // 2

    def rows(ci, _):
        r0 = pl.multiple_of(ci * chunk, chunk)
        blk = zp_scr[pl.ds(r0, chunk + 2 * HALO), :]
        acc = jnp.zeros((chunk, GROUP_W), F32)
        for k in range(CONV_A_WIDTH):
            off = HALO - half_a + k
            acc = acc + caw_ref[k:k + 1, :] * blk[off:off + chunk, :]
        z = acc + cab_ref[...]
        mu = jnp.mean(z, axis=-1, keepdims=True)
        zc = z - mu
        var = jnp.mean(zc * zc, axis=-1, keepdims=True)
        ln = zc * lax.rsqrt(var + EPS) * lng_ref[...] + lnb_ref[...]
        ya_ref[pl.ds(r0, chunk), :] = (ln * _sigmoid(ln)).astype(ya_ref.dtype)
        mblk = mp_scr[pl.ds(r0, chunk + 2 * HALO), :]
        cacc = jnp.zeros((chunk, GROUP_W), F32)
        for k in range(CONV_C_WIDTH):
            off = HALO - half_c + k
            cacc = cacc + ccw_ref[k:k + 1, :] * mblk[off:off + chunk, :]
        y = bg_ref[pl.ds(r0, chunk), :].astype(F32) * cacc
        inv = lax.rsqrt(jnp.mean(y * y, axis=-1, keepdims=True) + EPS)
        ys_ref[pl.ds(r0, chunk), :] = (y * inv * ncg_g_ref[...]).astype(ys_ref.dtype)
        return 0

    lax.fori_loop(0, tr // chunk, rows, 0)


def _convmix(u, conv_a_w, conv_a_b, ln_a_g, ln_a_b, conv_c_w, norm_c_g, *,
             n_rows, n_lat_rows, seq, ctx_len, tr):
    t_all = u.shape[0]
    nlt = n_lat_rows // tr
    hb = tr // HALO
    last_h = t_all // HALO - 1

    def cur(col):
        return pl.BlockSpec((tr, GROUP_W), lambda i: (i, col))

    def prev(col):
        return pl.BlockSpec((HALO, GROUP_W), lambda i: (jnp.maximum(i * hb - 1, 0), col))

    def nxt(col):
        return pl.BlockSpec((HALO, GROUP_W), lambda i: (jnp.minimum((i + 1) * hb, last_h), col))

    def small(r):
        return pl.BlockSpec((r, GROUP_W), lambda i: (0, 0))

    body = functools.partial(_convmix_body, nlt=nlt, per_lat=seq // tr,
                             per_ctx=max(ctx_len // tr, 1), chunk=64)
    return pl.pallas_call(
        body,
        grid=(n_rows // tr,),
        in_specs=[cur(COL_VAL), cur(COL_GATE), prev(COL_VAL), prev(COL_GATE), nxt(COL_VAL), nxt(COL_GATE),
                  cur(COL_BG), cur(COL_CG), cur(COL_VC), prev(COL_CG), prev(COL_VC), nxt(COL_CG), nxt(COL_VC),
                  small(CONV_A_WIDTH), small(1), small(1), small(1), small(CONV_C_WIDTH), small(1)],
        out_specs=[pl.BlockSpec((tr, GROUP_W), lambda i: (i, 0)),
                   pl.BlockSpec((tr, GROUP_W), lambda i: (i, 0))],
        out_shape=[jax.ShapeDtypeStruct((n_rows, GROUP_W), BF16),
                   jax.ShapeDtypeStruct((n_rows, GROUP_W), BF16)],
        scratch_shapes=[pltpu.VMEM((tr + 2 * HALO, GROUP_W), F32),
                        pltpu.VMEM((tr + 2 * HALO, GROUP_W), F32)],
        compiler_params=_params(("arbitrary",)),
        name="convmix",
    )(*([u] * 13), conv_a_w, conv_a_b.reshape(1, -1), ln_a_g.reshape(1, -1), ln_a_b.reshape(1, -1),
      conv_c_w, norm_c_g.reshape(1, -1))


def _rope(x, cos, sin_signed, low):
    rot = jnp.where(low, pltpu.roll(x, LANES - 16, axis=1), pltpu.roll(x, 16, axis=1))
    return x * cos + rot * sin_signed


def _attn_body(q_ref, kl_ref, vl_ref, kc_ref, vc_ref, cosq_ref, sinq_ref, cosk_ref, sink_ref,
               lam_ref, g_ref, o_ref, k_scr, v_scr, *, nq, ctx_len, lam_init):
    i = pl.program_id(2)
    tq = q_ref.shape[0]
    lane = lax.broadcasted_iota(I32, (1, LANES), 1)
    low = (lane % 32) < 16
    first = lane < HEAD_DIM

    @pl.when(i == 0)
    def _():
        k_scr[0:ctx_len, :] = kc_ref[...]
        v_scr[0:ctx_len, :] = vc_ref[...]
        kl = _rope(kl_ref[...].astype(F32), cosk_ref[...], sink_ref[...], low)
        k_scr[ctx_len:, :] = kl.astype(BF16)
        v_scr[ctx_len:, :] = vl_ref[...]

    lv = lam_ref[...]
    lam = (jnp.exp(jnp.sum(lv[0:1, :] * lv[1:2, :], axis=-1, keepdims=True))
           - jnp.exp(jnp.sum(lv[2:3, :] * lv[3:4, :], axis=-1, keepdims=True)) + lam_init)

    def attend(q, k, v):
        scale = HEAD_DIM ** -0.5

        def probs(qh):
            s = lax.dot_general(qh.astype(BF16), k, (((1,), (1,)), ((), ())),
                                preferred_element_type=F32) * scale
            e = jnp.exp(s - jnp.max(s, axis=-1, keepdims=True))
            return e / jnp.sum(e, axis=-1, keepdims=True)

        p1 = probs(jnp.where(first, q, 0.0))
        p2 = probs(jnp.where(first, 0.0, q))
        w = (p1 - lam * p2).astype(BF16)
        o = jnp.dot(w, v, preferred_element_type=F32)
        inv = lax.rsqrt(jnp.mean(o * o, axis=-1, keepdims=True) + EPS)
        o_ref[...] = ((o * inv * g_ref[...]) * (1.0 - lam_init)).astype(o_ref.dtype)

    @pl.when(i < nq)
    def _():
        q = _rope(q_ref[...].astype(F32), cosq_ref[...], sinq_ref[...], low)
        attend(q, k_scr[...], v_scr[...])

    @pl.when(i >= nq)
    def _():
        attend(q_ref[...].astype(F32), k_scr[0:ctx_len, :], v_scr[0:ctx_len, :])


def _attention(u, cos2, sin2, diff_lambda, diff_norm_g, *, bsz, seq, ctx_len, need_ctx, lam_init):
    tq = ctx_len
    nq = seq // tq
    n_rows = bsz * seq + (bsz * ctx_len if need_ctx else 0)
    hq, hk, hv = (COL_Q * GROUP_W // LANES, COL_K * GROUP_W // LANES, COL_V * GROUP_W // LANES)

    def qrow(b, i):
        return jnp.where(i < nq, b * nq + i, bsz * nq + b)

    body = functools.partial(_attn_body, nq=nq, ctx_len=ctx_len, lam_init=lam_init)
    return pl.pallas_call(
        body,
        grid=(bsz, DIFF_HEADS, nq + (1 if need_ctx else 0)),
        in_specs=[
            pl.BlockSpec((tq, LANES), lambda b, h, i: (qrow(b, i), hq + h)),
            pl.BlockSpec((seq, LANES), lambda b, h, i: (b, hk + h)),
            pl.BlockSpec((seq, LANES), lambda b, h, i: (b, hv + h)),
            pl.BlockSpec((ctx_len, LANES), lambda b, h, i: (bsz * nq + b, hk + h)),
            pl.BlockSpec((ctx_len, LANES), lambda b, h, i: (bsz * nq + b, hv + h)),
            pl.BlockSpec((tq, LANES), lambda b, h, i: (jnp.minimum(i, nq - 1), 0)),
            pl.BlockSpec((tq, LANES), lambda b, h, i: (jnp.minimum(i, nq - 1), 0)),
            pl.BlockSpec((seq, LANES), lambda b, h, i: (0, 0)),
            pl.BlockSpec((seq, LANES), lambda b, h, i: (0, 0)),
            pl.BlockSpec((4, HEAD_DIM), lambda b, h, i: (0, 0)),
            pl.BlockSpec((1, LANES), lambda b, h, i: (0, 0)),
        ],
        out_specs=pl.BlockSpec((tq, LANES), lambda b, h, i: (qrow(b, i), h)),
        out_shape=jax.ShapeDtypeStruct((n_rows, GROUP_W), BF16),
        scratch_shapes=[pltpu.VMEM((ctx_len + seq, LANES), BF16),
                        pltpu.VMEM((ctx_len + seq, LANES), BF16)],
        compiler_params=_params(("arbitrary", "arbitrary", "arbitrary")),
        name="diff_attention",
    )(u, u, u, u, u, cos2, sin2, cos2, sin2, diff_lambda, diff_norm_g.reshape(1, -1))


def _lru_body(recl_ref, gatel_ref, recc_ref, gatec_ref, cw_ref, cb_ref, wa_ref, ba_ref,
              wx_ref, bx_ref, lam_ref, ng_ref, o_ref,
              xp_scr, a_scr, b_scr, hs_scr, y_scr, *, nl, seq, ctx_len, need_ctx):
    j = pl.program_id(1)
    pad = 8
    sub = lax.broadcasted_iota(I32, (8, GROUP_W), 0)

    def coefficients(rec_ref, n, row0, d):
        xp_scr[...] = jnp.zeros_like(xp_scr)
        xp_scr[pad:pad + n, :] = rec_ref[...].astype(F32)
        base = pad if d == 1 else pad - (LRU_CONV_WIDTH - 1)
        chunk = 128

        def rows(ci, _):
            r0 = pl.multiple_of(ci * chunk, chunk)
            blk = xp_scr[pl.ds(r0, chunk + 2 * pad), :]
            xcv = jnp.zeros((chunk, GROUP_W), F32)
            for k in range(LRU_CONV_WIDTH):
                xcv = xcv + cw_ref[d, k:k + 1, :] * blk[base + k:base + k + chunk, :]
            xcv = xcv + cb_ref[d]
            xb = xcv.astype(BF16)
            r = _sigmoid(jnp.dot(xb, wa_ref[d], preferred_element_type=F32) + ba_ref[d])
            ig = _sigmoid(jnp.dot(xb, wx_ref[d], preferred_element_type=F32) + bx_ref[d])
            log_a = -LRU_C * r * jax.nn.softplus(-lam_ref[d])
            a_scr[pl.ds(row0 + r0, chunk), :] = jnp.exp(log_a)
            b_scr[pl.ds(row0 + r0, chunk), :] = jnp.sqrt(1.0 - jnp.exp(2.0 * log_a)) * (ig * xcv)
            return 0

        lax.fori_loop(0, n // chunk, rows, 0)

    def scan(row0, n, h, d):
        nb = n // 8

        def blk(bi, h):
            bidx = bi if d == 0 else nb - 1 - bi
            r0 = pl.multiple_of(row0 + bidx * 8, 8)
            ab = a_scr[pl.ds(r0, 8), :]
            bb = b_scr[pl.ds(r0, 8), :]
            out = jnp.zeros((8, GROUP_W), F32)
            for s in (range(8) if d == 0 else range(7, -1, -1)):
                hn = ab * h + bb
                h = jnp.broadcast_to(hn[s:s + 1, :], (8, GROUP_W))
                out = jnp.where(sub == s, hn, out)
            if d == 0:
                hs_scr[pl.ds(r0, 8), :] = out
            else:
                hs_scr[pl.ds(r0, 8), :] = hs_scr[pl.ds(r0, 8), :] + out
            return h

        return lax.fori_loop(0, nb, blk, h)

    @pl.when(j == 0)
    def _():
        for d in range(2):
            coefficients(recc_ref, ctx_len, 0, d)
            coefficients(recl_ref, seq, ctx_len, d)
            h = jnp.zeros((8, GROUP_W), F32)
            h = scan(0, ctx_len, h, d)
            scan(ctx_len, seq, h, d)

        def finish(gate_ref, row0, n):
            gte = gate_ref[...].astype(F32)
            y = jax.nn.gelu(gte) * hs_scr[row0:row0 + n, :]
            inv = lax.rsqrt(jnp.mean(y * y, axis=-1, keepdims=True) + EPS)
            y_scr[row0:row0 + n, :] = (y * inv * ng_ref[...]).astype(y_scr.dtype)

        finish(gatel_ref, ctx_len, seq)
        if need_ctx:
            finish(gatec_ref, 0, ctx_len)

    tl = o_ref.shape[0]
    src = jnp.where(j < nl, ctx_len + j * tl, 0)
    o_ref[...] = y_scr[pl.ds(pl.multiple_of(src, tl), tl), :]


def _rglru(u, conv_w, conv_b, wa_bd, ba, wx_bd, bx, lam, norm_g, *, bsz, seq, ctx_len, need_ctx):
    tl = ctx_len
    nl = seq // tl
    n_rows = bsz * seq + (bsz * ctx_len if need_ctx else 0)

    def orow(b, j):
        return jnp.where(j < nl, b * nl + j, bsz * nl + b)

    def full(shape):
        return pl.BlockSpec(shape, lambda b, j: (0,) * len(shape))

    body = functools.partial(_lru_body, nl=nl, seq=seq, ctx_len=ctx_len, need_ctx=need_ctx)
    n = ctx_len + seq
    return pl.pallas_call(
        body,
        grid=(bsz, nl + (1 if need_ctx else 0)),
        in_specs=[
            pl.BlockSpec((seq, GROUP_W), lambda b, j: (b, COL_REC)),
            pl.BlockSpec((seq, GROUP_W), lambda b, j: (b, COL_GD)),
            pl.BlockSpec((ctx_len, GROUP_W), lambda b, j: (bsz * nl + b, COL_REC)),
            pl.BlockSpec((ctx_len, GROUP_W), lambda b, j: (bsz * nl + b, COL_GD)),
            full((2, LRU_CONV_WIDTH, GROUP_W)), full((2, 1, GROUP_W)),
            full((2, GROUP_W, GROUP_W)), full((2, 1, GROUP_W)),
            full((2, GROUP_W, GROUP_W)), full((2, 1, GROUP_W)),
            full((2, 1, GROUP_W)), full((1, GROUP_W)),
        ],
        out_specs=pl.BlockSpec((tl, GROUP_W), lambda b, j: (orow(b, j), 0)),
        out_shape=jax.ShapeDtypeStruct((n_rows, GROUP_W), BF16),
        scratch_shapes=[pltpu.VMEM((seq + 16, GROUP_W), F32),
                        pltpu.VMEM((n, GROUP_W), F32), pltpu.VMEM((n, GROUP_W), F32),
                        pltpu.VMEM((n, GROUP_W), F32), pltpu.VMEM((n, GROUP_W), BF16)],
        compiler_params=_params(("arbitrary", "arbitrary")),
        name="rglru",
    )(u, u, u, u, conv_w, conv_b.reshape(2, 1, -1), wa_bd, ba.reshape(2, 1, -1),
      wx_bd, bx.reshape(2, 1, -1), lam.reshape(2, 1, -1), norm_g.reshape(1, -1))


def _pack_rows(h, o_ref):
    tm, d = h.shape
    half = d // 2
    words = half // LANES

    def bits(v):
        return pltpu.bitcast(v.astype(BF16).astype(F32), U32)

    for c in range(words):
        lo = bits(h[:, c * LANES:(c + 1) * LANES]) >> 16
        hi = bits(h[:, half + c * LANES:half + (c + 1) * LANES]) & jnp.uint32(0xFFFF0000)
        o_ref[pl.ds(c, tm, stride=words), :] = hi | lo


def _outproj_body(ya_ref, yb_ref, ys_ref, yd_ref, w_ref, x_ref, g1_ref, sc_ref, sh_ref, ng_ref,
                  rw_ref, rb_ref, xo_ref, hp_ref, eid_ref, gate_ref, rank_ref, cnt_ref, cnt_scr):
    i = pl.program_id(0)
    tm = x_ref.shape[0]

    @pl.when(i == 0)
    def _():
        cnt_scr[...] = jnp.zeros_like(cnt_scr)

    acc = jnp.dot(ya_ref[...], w_ref[0:GROUP_W, :], preferred_element_type=F32)
    acc = acc + jnp.dot(yb_ref[...], w_ref[GROUP_W:2 * GROUP_W, :], preferred_element_type=F32)
    acc = acc + jnp.dot(ys_ref[...], w_ref[2 * GROUP_W:3 * GROUP_W, :], preferred_element_type=F32)
    acc = acc + jnp.dot(yd_ref[...], w_ref[3 * GROUP_W:4 * GROUP_W, :], preferred_element_type=F32)
    x = x_ref[...] + g1_ref[0] * acc
    xo_ref[...] = x
    inv = lax.rsqrt(jnp.mean(x * x, axis=-1, keepdims=True) + EPS)
    h2 = (x * inv * ng_ref[...]) * (1.0 + sc_ref[0]) + sh_ref[0]
    _pack_rows(h2, hp_ref)

    logits = jnp.dot(h2, rw_ref[...], preferred_element_type=F32,
                     precision=lax.Precision.HIGHEST) + rb_ref[...]
    lane = lax.broadcasted_iota(I32, (tm, N_EXPERTS), 1).astype(F32)
    l = logits
    vals, sels = [], []
    for k in range(TOP_K):
        m = jnp.max(l, axis=-1, keepdims=True)
        idx = jnp.min(jnp.where(l == m, lane, float(N_EXPERTS)), axis=-1, keepdims=True)
        sel = lane == idx
        vals.append(m)
        sels.append(sel)
        eid_ref[:, k:k + 1] = idx.astype(I32)
        l = jnp.where(sel, -jnp.inf, l)
    es = [jnp.exp(v - vals[0]) for v in vals]
    den = es[0] + es[1] + es[2] + es[3]
    for k in range(TOP_K):
        gate_ref[:, k:k + 1] = es[k] / den

    onehots = [s.astype(F32) for s in sels]
    m_all = onehots[0] + onehots[1] + onehots[2] + onehots[3]
    r_i = lax.broadcasted_iota(I32, (tm, tm), 0)
    c_i = lax.broadcasted_iota(I32, (tm, tm), 1)
    tri = (c_i < r_i).astype(BF16)
    before = jnp.dot(tri, m_all.astype(BF16), preferred_element_type=F32) + cnt_scr[...]
    for k in range(TOP_K):
        rank_ref[:, k:k + 1] = jnp.sum(onehots[k] * before, axis=-1, keepdims=True).astype(I32)
    cnt_scr[...] = cnt_scr[...] + jnp.sum(m_all, axis=0, keepdims=True)
    cnt_ref[...] = cnt_scr[...]


def _outproj(ya, yb, ys, yd, w_out, x, mod3, norm_g, router_w, router_b, *,
             n_rows, n_lat_rows, seq, bsz, tm):
    d = x.shape[1]
    nlt = n_lat_rows // tm
    per_seq = seq // tm
    words = d // 2 // LANES

    def mrow(i):
        return jnp.where(i < nlt, i // per_seq, bsz)

    def yspec():
        return pl.BlockSpec((tm, GROUP_W), lambda i: (i, 0))

    def mspec(chunk):
        return pl.BlockSpec((1, 1, d), lambda i: (mrow(i), 0, chunk))

    def kspec():
        return pl.BlockSpec((tm, TOP_K), lambda i: (i, 0))

    return pl.pallas_call(
        _outproj_body,
        grid=(n_rows // tm,),
        in_specs=[yspec(), yspec(), yspec(), yspec(),
                  pl.BlockSpec((4 * GROUP_W, d), lambda i: (0, 0)),
                  pl.BlockSpec((tm, d), lambda i: (i, 0)),
                  mspec(2), mspec(4), mspec(3),
                  pl.BlockSpec((1, d), lambda i: (0, 0)),
                  pl.BlockSpec((d, N_EXPERTS), lambda i: (0, 0)),
                  pl.BlockSpec((1, N_EXPERTS), lambda i: (0, 0))],
        out_specs=[pl.BlockSpec((tm, d), lambda i: (i, 0)),
                   pl.BlockSpec((tm * words, LANES), lambda i: (i, 0)),
                   kspec(), kspec(), kspec(),
                   pl.BlockSpec((1, N_EXPERTS), lambda i: (0, 0))],
        out_shape=[jax.ShapeDtypeStruct((n_rows, d), F32),
                   jax.ShapeDtypeStruct((n_rows * words, LANES), U32),
                   jax.ShapeDtypeStruct((n_rows, TOP_K), I32),
                   jax.ShapeDtypeStruct((n_rows, TOP_K), F32),
                   jax.ShapeDtypeStruct((n_rows, TOP_K), I32),
                   jax.ShapeDtypeStruct((1, N_EXPERTS), F32)],
        scratch_shapes=[pltpu.VMEM((1, N_EXPERTS), F32)],
        compiler_params=_params(("arbitrary",)),
        name="outproj_route",
    )(ya, yb, ys, yd, w_out, x, mod3, mod3, mod3, norm_g.reshape(1, d), router_w,
      router_b.reshape(1, N_EXPERTS))


def _dispatch_body(dest_ref, hp_hbm, xs_in_hbm, xs_hbm, sem, *, tm, words):
    del xs_in_hbm
    i = pl.program_id(0)

    def row_copy(t, dst):
        return pltpu.make_async_copy(
            hp_hbm.at[pl.ds(pl.multiple_of(t * words, words), words), :],
            xs_hbm.at[pl.ds(pl.multiple_of(dst * words, words), words), :], sem)

    def issue(r, _):
        for k in range(TOP_K):
            row_copy(i * tm + r, dest_ref[r * TOP_K + k]).start()
        return 0

    lax.fori_loop(0, tm, issue, 0)

    def drain(r, _):
        for k in range(TOP_K):
            row_copy(0, 0).wait()
        return 0

    lax.fori_loop(0, tm, drain, 0)


def _dispatch(dest, hp, xs_init, *, n_tok, tm, words):
    body = functools.partial(_dispatch_body, tm=tm, words=words)
    return pl.pallas_call(
        body,
        grid=(n_tok // tm,),
        in_specs=[pl.BlockSpec((tm * TOP_K,), lambda i: (i,), memory_space=pltpu.SMEM),
                  pl.BlockSpec(memory_space=pl.ANY),
                  pl.BlockSpec(memory_space=pl.ANY)],
        out_specs=pl.BlockSpec(memory_space=pl.ANY),
        out_shape=jax.ShapeDtypeStruct(xs_init.shape, xs_init.dtype),
        scratch_shapes=[pltpu.SemaphoreType.DMA(())],
        input_output_aliases={2: 0},
        compiler_params=_params(("arbitrary",)),
        name="moe_dispatch",
    )(dest, hp, xs_init)


def _ffn_body(te_ref, nu_ref, xs_ref, w1g_ref, w1u_ref, b1g_ref, b1u_ref, w2_ref, b2_ref,
              ys_ref, x_scr, *, tm, d):
    r = pl.program_id(0)
    half = d // 2
    words = half // LANES
    out_words = d // LANES

    @pl.when(r < nu_ref[0])
    def _():
        for c in range(words):
            w = xs_ref[pl.ds(c, tm, stride=words), :]
            lo = pltpu.bitcast(w << 16, F32)
            hi = pltpu.bitcast(w & jnp.uint32(0xFFFF0000), F32)
            x_scr[:, c * LANES:(c + 1) * LANES] = lo.astype(BF16)
            x_scr[:, half + c * LANES:half + (c + 1) * LANES] = hi.astype(BF16)
        x = x_scr[...]
        g = jnp.dot(x, w1g_ref[...], preferred_element_type=F32) + b1g_ref[...]
        up = jnp.dot(x, w1u_ref[...], preferred_element_type=F32) + b1u_ref[...]
        g = jnp.minimum(g, SWIGLU_LIMIT)
        up = jnp.clip(up, -SWIGLU_LIMIT, SWIGLU_LIMIT)
        act = (up + 1.0) * (g * _sigmoid(SWIGLU_ALPHA * g))
        y = jnp.dot(act.astype(BF16), w2_ref[...], preferred_element_type=F32) + b2_ref[...]
        for c in range(out_words):
            ys_ref[pl.ds(c, tm, stride=out_words), :] = y[:, c * LANES:(c + 1) * LANES]

    @pl.when(r >= nu_ref[0])
    def _():
        ys_ref[...] = jnp.zeros_like(ys_ref)


def _expert_ffn(tile_expert, n_used, xs, w1g, w1u, b1g, b1u, w2, b2, *, n_tiles, tm, d):
    f = w1g.shape[2]
    words = d // 2 // LANES
    out_words = d // LANES

    def row(r, te, nu):
        return (jnp.minimum(r, nu[0] - 1), 0)

    def wspec(shape):
        return pl.BlockSpec((None,) + shape, lambda r, te, nu: (te[r], 0, 0))

    body = functools.partial(_ffn_body, tm=tm, d=d)
    return pl.pallas_call(
        body,
        grid_spec=pltpu.PrefetchScalarGridSpec(
            num_scalar_prefetch=2,
            grid=(n_tiles,),
            in_specs=[pl.BlockSpec((tm * words, LANES), row),
                      wspec((d, f)), wspec((d, f)), wspec((1, f)), wspec((1, f)),
                      wspec((f, d)), wspec((1, d))],
            out_specs=pl.BlockSpec((tm * out_words, LANES), lambda r, te, nu: (r, 0)),
            scratch_shapes=[pltpu.VMEM((tm, d), BF16)]),
        out_shape=jax.ShapeDtypeStruct((n_tiles * tm * out_words, LANES), F32),
        compiler_params=_params(("arbitrary",)),
        name="moe_expert_ffn",
    )(tile_expert, n_used, xs, w1g, w1u, b1g, b1u, w2, b2)


def _combine_body(dest_ref, gate_ref, x_ref, g2_ref, fg_ref, ys_hbm, o_ref, buf, sem,
                  *, tc, d, final):
    out_words = d // LANES

    def row_copy(src, slot):
        return pltpu.make_async_copy(
            ys_hbm.at[pl.ds(pl.multiple_of(src * out_words, out_words), out_words), :],
            buf.at[pl.ds(pl.multiple_of(slot * out_words, out_words), out_words), :], sem)

    def issue(r, _):
        for k in range(TOP_K):
            row_copy(dest_ref[r * TOP_K + k], k * tc + r).start()
        return 0

    lax.fori_loop(0, tc, issue, 0)

    def drain(r, _):
        for k in range(TOP_K):
            row_copy(0, 0).wait()
        return 0

    lax.fori_loop(0, tc, drain, 0)

    gates = gate_ref[...]
    for c in range(out_words):
        f = jnp.zeros((tc, LANES), F32)
        for k in range(TOP_K):
            rows = buf[pl.ds(k * tc * out_words + c, tc, stride=out_words), :]
            f = f + gates[:, k:k + 1] * rows
        cols = slice(c * LANES, (c + 1) * LANES)
        o_ref[:, cols] = x_ref[:, cols] + g2_ref[0][:, cols] * f
    if final:
        x = o_ref[...]
        inv = lax.rsqrt(jnp.mean(x * x, axis=-1, keepdims=True) + EPS)
        o_ref[...] = x * inv * fg_ref[...]


def _combine(dest, gate, x, mod3, final_g, ys, *, n_tok, n_lat_rows, seq, bsz, tc, final):
    d = x.shape[1]
    nlt = n_lat_rows // tc
    per_seq = seq // tc
    out_words = d // LANES

    def mrow(i):
        return jnp.where(i < nlt, i // per_seq, bsz)

    body = functools.partial(_combine_body, tc=tc, d=d, final=final)
    return pl.pallas_call(
        body,
        grid=(n_tok // tc,),
        in_specs=[pl.BlockSpec((tc * TOP_K,), lambda i: (i,), memory_space=pltpu.SMEM),
                  pl.BlockSpec((tc, TOP_K), lambda i: (i, 0)),
                  pl.BlockSpec((tc, d), lambda i: (i, 0)),
                  pl.BlockSpec((1, 1, d), lambda i: (mrow(i), 0, 5)),
                  pl.BlockSpec((1, d), lambda i: (0, 0)),
                  pl.BlockSpec(memory_space=pl.ANY)],
        out_specs=pl.BlockSpec((tc, d), lambda i: (i, 0)),
        out_shape=jax.ShapeDtypeStruct((n_tok, d), F32),
        scratch_shapes=[pltpu.VMEM((TOP_K * tc * out_words, LANES), F32),
                        pltpu.SemaphoreType.DMA(())],
        compiler_params=_params(("arbitrary",)),
        name="moe_combine",
    )(dest, gate, x, mod3, final_g.reshape(1, d), ys)


def _rope_tables(seq):
    rows = seq // GRID_W
    row = jnp.repeat(jnp.arange(rows, dtype=F32), GRID_W)
    col = jnp.tile(jnp.arange(GRID_W, dtype=F32), rows)
    half = HEAD_DIM // 2
    inv = ROPE_BASE ** (-jnp.arange(0, half, 2, dtype=F32) / half)
    ar = row[:, None] * inv
    ac = col[:, None] * inv
    ang = jnp.concatenate([ar, ar, ac, ac], axis=-1)
    ang = jnp.concatenate([ang, ang], axis=-1)
    low = (jnp.arange(LANES) % 32) < 16
    return jnp.cos(ang), jnp.where(low, -jnp.sin(ang), jnp.sin(ang))


def _block_diag(w):
    nd, nb, hi, ho = w.shape
    eye = jnp.eye(nb, dtype=w.dtype)
    return jnp.einsum('dhij,hg->dhigj', w, eye).reshape(nd, nb * hi, nb * ho)


def _moe_plan(eid, rank, counts, *, tm, n_tiles):
    cnt = counts.reshape(N_EXPERTS).astype(I32)
    padded = ((cnt + tm - 1) // tm) * tm
    ends = jnp.cumsum(padded)
    base = ends - padded
    dest = (base[eid] + rank).reshape(-1).astype(I32)
    tile_expert = jnp.searchsorted(ends // tm, jnp.arange(n_tiles, dtype=I32), side='right')
    tile_expert = jnp.minimum(tile_expert, N_EXPERTS - 1).astype(I32)
    n_used = (ends[-1] // tm).reshape(1).astype(I32)
    return dest, tile_expert, n_used


def kernel(x, c, ctx, c_ctx, ada_w, ada_b, norm1_g, norm2_g, w_in, w_out, conv_a_w, conv_a_b, ln_a_g, ln_a_b, diff_lambda, diff_norm_g, conv_c_w, norm_c_g, lru_conv_w, lru_conv_b, lru_wa, lru_ba, lru_wx, lru_bx, lru_lam, norm_d_g, router_w, router_b, exp_w1, exp_b1, exp_w2, exp_b2, final_g):
    bsz, seq, d = x.shape
    ctx_len = ctx.shape[1]
    depth = ada_w.shape[0]
    n_lat = bsz * seq
    n_ctx = bsz * ctx_len
    t_all = n_lat + n_ctx
    assert bsz + 1 <= MOD_ROWS and seq % ctx_len == 0 and ctx_len % 128 == 0
    tm_in = min(1024, seq, n_ctx)
    assert seq % tm_in == 0 and n_ctx % tm_in == 0
    tr = ctx_len if ctx_len <= 256 else 256
    assert ctx_len % tr == 0
    tm_ffn = 256
    tc = 128
    words = d // 2 // LANES

    cc = jnp.zeros((MOD_ROWS, d), F32).at[:bsz].set(c).at[bsz].set(c_ctx)
    mod = _modulation(cc, ada_w, ada_b)
    cos2, sin2 = _rope_tables(seq)
    xa = jnp.concatenate([x.reshape(n_lat, d), ctx.reshape(n_ctx, d)], axis=0)

    for i in range(depth):
        need_ctx = i < depth - 1
        lam_init = 0.8 - 0.6 * math.exp(-0.3 * i)
        n_rows = t_all if need_ctx else n_lat
        mod3 = mod[i].reshape(MOD_ROWS, 1, 6 * d)

        u = _inproj(xa, mod3, norm1_g[i], w_in[i].astype(BF16),
                    n_lat_rows=n_lat, seq=seq, bsz=bsz, tm=tm_in)
        ya, ys = _convmix(u, conv_a_w[i], conv_a_b[i], ln_a_g[i], ln_a_b[i], conv_c_w[i], norm_c_g[i],
                          n_rows=n_rows, n_lat_rows=n_lat, seq=seq, ctx_len=ctx_len, tr=tr)
        yb = _attention(u, cos2, sin2, diff_lambda[i], diff_norm_g[i], bsz=bsz, seq=seq,
                        ctx_len=ctx_len, need_ctx=need_ctx, lam_init=lam_init)
        yd = _rglru(u, lru_conv_w[i], lru_conv_b[i], _block_diag(lru_wa[i]).astype(BF16), lru_ba[i],
                    _block_diag(lru_wx[i]).astype(BF16), lru_bx[i], lru_lam[i], norm_d_g[i],
                    bsz=bsz, seq=seq, ctx_len=ctx_len, need_ctx=need_ctx)
        xm, hp, eid, gate, rank, counts = _outproj(
            ya, yb, ys, yd, w_out[i].astype(BF16), xa, mod3, norm2_g[i], router_w[i], router_b[i],
            n_rows=n_rows, n_lat_rows=n_lat, seq=seq, bsz=bsz, tm=tr)

        n_tiles = (n_rows * TOP_K) // tm_ffn + N_EXPERTS
        dest, tile_expert, n_used = _moe_plan(eid, rank, counts, tm=tm_ffn, n_tiles=n_tiles)
        xs = _dispatch(dest, hp, jnp.zeros((n_tiles * tm_ffn * words, LANES), U32),
                       n_tok=n_rows, tm=tr, words=words)
        w1 = exp_w1[i]
        ff = w1.shape[2] // 2
        ysort = _expert_ffn(
            tile_expert, n_used, xs,
            w1[:, :, 0::2].astype(BF16), w1[:, :, 1::2].astype(BF16),
            exp_b1[i][:, 0::2].reshape(N_EXPERTS, 1, ff), exp_b1[i][:, 1::2].reshape(N_EXPERTS, 1, ff),
            exp_w2[i].astype(BF16), exp_b2[i].reshape(N_EXPERTS, 1, d),
            n_tiles=n_tiles, tm=tm_ffn, d=d)
        xa = _combine(dest, gate, xm, mod3, final_g, ysort, n_tok=n_rows, n_lat_rows=n_lat,
                      seq=seq, bsz=bsz, tc=tc, final=not need_ctx)

    return xa[:n_lat].reshape(bsz, seq, d)
```

```python
import functools
import math

import jax
import jax.numpy as jnp
from jax import lax
from jax.experimental import pallas as pl
from jax.experimental.pallas import tpu as pltpu

F32 = jnp.float32
BF16 = jnp.bfloat16
I32 = jnp.int32
U32 = jnp.uint32

GRID_W = 64
GROUP_W = 512
HEAD_DIM = 64
DIFF_HEADS = 4
CONV_A_WIDTH = 31
CONV_C_WIDTH = 3
LRU_CONV_WIDTH = 4
LRU_C = 8.0
ROPE_BASE = 10000.0
N_EXPERTS = 32
TOP_K = 4
SWIGLU_LIMIT = 7.0
SWIGLU_ALPHA = 1.702
EPS = 1e-6

LANES = 128
HALO = 16
MOD_ROWS = 16
VMEM_LIMIT = 56 * 1024 * 1024

COL_VAL, COL_GATE, COL_Q, COL_K, COL_V, COL_BG, COL_CG, COL_VC, COL_GD, COL_REC = range(10)


def _params(sem, vmem=VMEM_LIMIT):
    return pltpu.CompilerParams(dimension_semantics=sem, vmem_limit_bytes=vmem)


def _sigmoid(x):
    return 1.0 / (1.0 + jnp.exp(-x))


def _mod_body(c_ref, w_ref, b_ref, o_ref):
    c = c_ref[...]
    s = (c * _sigmoid(c)).astype(BF16)
    o_ref[...] = jnp.dot(s, w_ref[...].astype(BF16), preferred_element_type=F32) + b_ref[...]


def _modulation(cc, ada_w, ada_b):
    depth, d, n = ada_w.shape
    tn = 1024
    return pl.pallas_call(
        _mod_body,
        grid=(depth, n // tn),
        in_specs=[
            pl.BlockSpec((MOD_ROWS, d), lambda l, j: (0, 0)),
            pl.BlockSpec((None, d, tn), lambda l, j: (l, 0, j)),
            pl.BlockSpec((None, 1, tn), lambda l, j: (l, 0, j)),
        ],
        out_specs=pl.BlockSpec((None, MOD_ROWS, tn), lambda l, j: (l, 0, j)),
        out_shape=jax.ShapeDtypeStruct((depth, MOD_ROWS, n), F32),
        compiler_params=_params(("arbitrary", "arbitrary")),
        name="modulation",
    )(cc, ada_w, ada_b.reshape(depth, 1, n))


def _inproj_body(x_ref, g_ref, sc_ref, sh_ref, w_ref, o_ref, h_scr, *, slab):
    j = pl.program_id(1)
    tm = x_ref.shape[0]

    @pl.when(j == 0)
    def _():
        g = g_ref[...]
        sc = 1.0 + sc_ref[0]
        sh = sh_ref[0]

        def rows(s, _):
            r0 = pl.multiple_of(s * slab, slab)
            x = x_ref[pl.ds(r0, slab), :]
            inv = lax.rsqrt(jnp.mean(x * x, axis=-1, keepdims=True) + EPS)
            h_scr[pl.ds(r0, slab), :] = ((x * inv * g) * sc + sh).astype(BF16)
            return 0

        lax.fori_loop(0, tm // slab, rows, 0)

    o_ref[...] = jnp.dot(h_scr[...], w_ref[...], preferred_element_type=F32).astype(o_ref.dtype)


def _inproj(x, mod3, norm_g, w_in, *, n_lat_rows, seq, bsz, tm):
    t_all, d = x.shape
    n_in = w_in.shape[1]
    tn = 1280
    nlt = n_lat_rows // tm
    per_seq = seq // tm

    def mrow(i):
        return jnp.where(i < nlt, i // per_seq, bsz)

    return pl.pallas_call(
        functools.partial(_inproj_body, slab=min(128, tm)),
        grid=(t_all // tm, n_in // tn),
        in_specs=[
            pl.BlockSpec((tm, d), lambda i, j: (i, 0)),
            pl.BlockSpec((1, d), lambda i, j: (0, 0)),
            pl.BlockSpec((1, 1, d), lambda i, j: (mrow(i), 0, 1)),
            pl.BlockSpec((1, 1, d), lambda i, j: (mrow(i), 0, 0)),
            pl.BlockSpec((d, tn), lambda i, j: (0, j)),
        ],
        out_specs=pl.BlockSpec((tm, tn), lambda i, j: (i, j)),
        out_shape=jax.ShapeDtypeStruct((t_all, n_in), BF16),
        scratch_shapes=[pltpu.VMEM((tm, d), BF16)],
        compiler_params=_params(("arbitrary", "arbitrary")),
        name="inproj",
    )(x, norm_g.reshape(1, d), mod3, mod3, w_in)


def _convmix_body(val_ref, gate_ref, pval_ref, pgate_ref, nval_ref, ngate_ref,
                  bg_ref, cg_ref, vc_ref, pcg_ref, pvc_ref, ncg_ref, nvc_ref,
                  caw_ref, cab_ref, lng_ref, lnb_ref, ccw_ref, ncg_g_ref,
                  ya_ref, ys_ref, zp_scr, mp_scr, *, nlt, per_lat, per_ctx, chunk):
    i = pl.program_id(0)
    tr = val_ref.shape[0]
    pos = jnp.where(i < nlt, i % per_lat, (i - nlt) % per_ctx)
    per = jnp.where(i < nlt, per_lat, per_ctx)
    keep_prev = (pos > 0).astype(F32)
    keep_next = (pos < per - 1).astype(F32)

    def glu(v_ref, g_ref):
        return v_ref[...].astype(F32) * _sigmoid(g_ref[...].astype(F32))

    zp_scr[0:HALO, :] = glu(pval_ref, pgate_ref) * keep_prev
    zp_scr[HALO:HALO + tr, :] = glu(val_ref, gate_ref)
    zp_scr[HALO + tr:, :] = glu(nval_ref, ngate_ref) * keep_next

    mp_scr[0:HALO, :] = pcg_ref[...].astype(F32) * pvc_ref[...].astype(F32) * keep_prev
    mp_scr[HALO:HALO + tr, :] = cg_ref[...].astype(F32) * vc_ref[...].astype(F32)
    mp_scr[HALO + tr:, :] = ncg_ref[...].astype(F32) * nvc_ref[...].astype(F32) * keep_next

    half_a = (CONV_A_WIDTH - 1) // 2
    half_c = (CONV_C_WIDTH - 1) // 2

    def rows(ci, _):
        r0 = pl.multiple_of(ci * chunk, chunk)
        blk = zp_scr[pl.ds(r0, chunk + 2 * HALO), :]
        acc = jnp.zeros((chunk, GROUP_W), F32)
        for k in range(CONV_A_WIDTH):
            off = HALO - half_a + k
            acc = acc + caw_ref[k:k + 1, :] * blk[off:off + chunk, :]
        z = acc + cab_ref[...]
        mu = jnp.mean(z, axis=-1, keepdims=True)
        zc = z - mu
        var = jnp.mean(zc * zc, axis=-1, keepdims=True)
        ln = zc * lax.rsqrt(var + EPS) * lng_ref[...] + lnb_ref[...]
        ya_ref[pl.ds(r0, chunk), :] = (ln * _sigmoid(ln)).astype(ya_ref.dtype)
        mblk = mp_scr[pl.ds(r0, chunk + 2 * HALO), :]
        cacc = jnp.zeros((chunk, GROUP_W), F32)
        for k in range(CONV_C_WIDTH):
            off = HALO - half_c + k
            cacc = cacc + ccw_ref[k:k + 1, :] * mblk[off:off + chunk, :]
        y = bg_ref[pl.ds(r0, chunk), :].astype(F32) * cacc
        inv = lax.rsqrt(jnp.mean(y * y, axis=-1, keepdims=True) + EPS)
        ys_ref[pl.ds(r0, chunk), :] = (y * inv * ncg_g_ref[...]).astype(ys_ref.dtype)
        return 0

    lax.fori_loop(0, tr // chunk, rows, 0)


def _convmix(u, conv_a_w, conv_a_b, ln_a_g, ln_a_b, conv_c_w, norm_c_g, *,
             n_rows, n_lat_rows, seq, ctx_len, tr):
    t_all = u.shape[0]
    nlt = n_lat_rows // tr
    hb = tr // HALO
    last_h = t_all // HALO - 1

    def cur(col):
        return pl.BlockSpec((tr, GROUP_W), lambda i: (i, col))

    def prev(col):
        return pl.BlockSpec((HALO, GROUP_W), lambda i: (jnp.maximum(i * hb - 1, 0), col))

    def nxt(col):
        return pl.BlockSpec((HALO, GROUP_W), lambda i: (jnp.minimum((i + 1) * hb, last_h), col))

    def small(r):
        return pl.BlockSpec((r, GROUP_W), lambda i: (0, 0))

    body = functools.partial(_convmix_body, nlt=nlt, per_lat=seq // tr,
                             per_ctx=max(ctx_len // tr, 1), chunk=64)
    return pl.pallas_call(
        body,
        grid=(n_rows // tr,),
        in_specs=[cur(COL_VAL), cur(COL_GATE), prev(COL_VAL), prev(COL_GATE), nxt(COL_VAL), nxt(COL_GATE),
                  cur(COL_BG), cur(COL_CG), cur(COL_VC), prev(COL_CG), prev(COL_VC), nxt(COL_CG), nxt(COL_VC),
                  small(CONV_A_WIDTH), small(1), small(1), small(1), small(CONV_C_WIDTH), small(1)],
        out_specs=[pl.BlockSpec((tr, GROUP_W), lambda i: (i, 0)),
                   pl.BlockSpec((tr, GROUP_W), lambda i: (i, 0))],
        out_shape=[jax.ShapeDtypeStruct((n_rows, GROUP_W), BF16),
                   jax.ShapeDtypeStruct((n_rows, GROUP_W), BF16)],
        scratch_shapes=[pltpu.VMEM((tr + 2 * HALO, GROUP_W), F32),
                        pltpu.VMEM((tr + 2 * HALO, GROUP_W), F32)],
        compiler_params=_params(("arbitrary",)),
        name="convmix",
    )(*([u] * 13), conv_a_w, conv_a_b.reshape(1, -1), ln_a_g.reshape(1, -1), ln_a_b.reshape(1, -1),
      conv_c_w, norm_c_g.reshape(1, -1))


def _rope(x, cos, sin_signed, low):
    rot = jnp.where(low, pltpu.roll(x, LANES - 16, axis=1), pltpu.roll(x, 16, axis=1))
    return x * cos + rot * sin_signed


def _attn_body(q_ref, kl_ref, vl_ref, kc_ref, vc_ref, cosq_ref, sinq_ref, cosk_ref, sink_ref,
               lam_ref, g_ref, o_ref, k_scr, v_scr, *, nq, ctx_len, lam_init):
    i = pl.program_id(2)
    tq = q_ref.shape[0]
    lane = lax.broadcasted_iota(I32, (1, LANES), 1)
    low = (lane % 32) < 16
    first = lane < HEAD_DIM

    @pl.when(i == 0)
    def _():
        k_scr[0:ctx_len, :] = kc_ref[...]
        v_scr[0:ctx_len, :] = vc_ref[...]
        kl = _rope(kl_ref[...].astype(F32), cosk_ref[...], sink_ref[...], low)
        k_scr[ctx_len:, :] = kl.astype(BF16)
        v_scr[ctx_len:, :] = vl_ref[...]

    lv = lam_ref[...]
    lam = (jnp.exp(jnp.sum(lv[0:1, :] * lv[1:2, :], axis=-1, keepdims=True))
           - jnp.exp(jnp.sum(lv[2:3, :] * lv[3:4, :], axis=-1, keepdims=True)) + lam_init)

    def attend(q, k, v):
        scale = HEAD_DIM ** -0.5

        def probs(qh):
            s = lax.dot_general(qh.astype(BF16), k, (((1,), (1,)), ((), ())),
                                preferred_element_type=F32) * scale
            e = jnp.exp(s - jnp.max(s, axis=-1, keepdims=True))
            return e / jnp.sum(e, axis=-1, keepdims=True)

        p1 = probs(jnp.where(first, q, 0.0))
        p2 = probs(jnp.where(first, 0.0, q))
        w = (p1 - lam * p2).astype(BF16)
        o = jnp.dot(w, v, preferred_element_type=F32)
        inv = lax.rsqrt(jnp.mean(o * o, axis=-1, keepdims=True) + EPS)
        o_ref[...] = ((o * inv * g_ref[...]) * (1.0 - lam_init)).astype(o_ref.dtype)

    @pl.when(i < nq)
    def _():
        q = _rope(q_ref[...].astype(F32), cosq_ref[...], sinq_ref[...], low)
        attend(q, k_scr[...], v_scr[...])

    @pl.when(i >= nq)
    def _():
        attend(q_ref[...].astype(F32), k_scr[0:ctx_len, :], v_scr[0:ctx_len, :])


def _attention(u, cos2, sin2, diff_lambda, diff_norm_g, *, bsz, seq, ctx_len, need_ctx, lam_init):
    tq = ctx_len
    nq = seq // tq
    n_rows = bsz * seq + (bsz * ctx_len if need_ctx else 0)
    hq, hk, hv = (COL_Q * GROUP_W // LANES, COL_K * GROUP_W // LANES, COL_V * GROUP_W // LANES)

    def qrow(b, i):
        return jnp.where(i < nq, b * nq + i, bsz * nq + b)

    body = functools.partial(_attn_body, nq=nq, ctx_len=ctx_len, lam_init=lam_init)
    return pl.pallas_call(
        body,
        grid=(bsz, DIFF_HEADS, nq + (1 if need_ctx else 0)),
        in_specs=[
            pl.BlockSpec((tq, LANES), lambda b, h, i: (qrow(b, i), hq + h)),
            pl.BlockSpec((seq, LANES), lambda b, h, i: (b, hk + h)),
            pl.BlockSpec((seq, LANES), lambda b, h, i: (b, hv + h)),
            pl.BlockSpec((ctx_len, LANES), lambda b, h, i: (bsz * nq + b, hk + h)),
            pl.BlockSpec((ctx_len, LANES), lambda b, h, i: (bsz * nq + b, hv + h)),
            pl.BlockSpec((tq, LANES), lambda b, h, i: (jnp.minimum(i, nq - 1), 0)),
            pl.BlockSpec((tq, LANES), lambda b, h, i: (jnp.minimum(i, nq - 1), 0)),
            pl.BlockSpec((seq, LANES), lambda b, h, i: (0, 0)),
            pl.BlockSpec((seq, LANES), lambda b, h, i: (0, 0)),
            pl.BlockSpec((4, HEAD_DIM), lambda b, h, i: (0, 0)),
            pl.BlockSpec((1, LANES), lambda b, h, i: (0, 0)),
        ],
        out_specs=pl.BlockSpec((tq, LANES), lambda b, h, i: (qrow(b, i), h)),
        out_shape=jax.ShapeDtypeStruct((n_rows, GROUP_W), BF16),
        scratch_shapes=[pltpu.VMEM((ctx_len + seq, LANES), BF16),
                        pltpu.VMEM((ctx_len + seq, LANES), BF16)],
        compiler_params=_params(("arbitrary", "arbitrary", "arbitrary")),
        name="diff_attention",
    )(u, u, u, u, u, cos2, sin2, cos2, sin2, diff_lambda, diff_norm_g.reshape(1, -1))


def _lru_body(recl_ref, gatel_ref, recc_ref, gatec_ref, cw_ref, cb_ref, wa_ref, ba_ref,
              wx_ref, bx_ref, lam_ref, ng_ref, o_ref,
              xp_scr, a_scr, b_scr, hs_scr, y_scr, *, nl, seq, ctx_len, need_ctx):
    j = pl.program_id(1)
    pad = 8
    sub = lax.broadcasted_iota(I32, (8, GROUP_W), 0)

    def coefficients(rec_ref, n, row0, d):
        xp_scr[...] = jnp.zeros_like(xp_scr)
        xp_scr[pad:pad + n, :] = rec_ref[...].astype(F32)
        base = pad if d == 1 else pad - (LRU_CONV_WIDTH - 1)
        chunk = 128

        def rows(ci, _):
            r0 = pl.multiple_of(ci * chunk, chunk)
            blk = xp_scr[pl.ds(r0, chunk + 2 * pad), :]
            xcv = jnp.zeros((chunk, GROUP_W), F32)
            for k in range(LRU_CONV_WIDTH):
                xcv = xcv + cw_ref[d, k:k + 1, :] * blk[base + k:base + k + chunk, :]
            xcv = xcv + cb_ref[d]
            xb = xcv.astype(BF16)
            r = _sigmoid(jnp.dot(xb, wa_ref[d], preferred_element_type=F32) + ba_ref[d])
            ig = _sigmoid(jnp.dot(xb, wx_ref[d], preferred_element_type=F32) + bx_ref[d])
            log_a = -LRU_C * r * jax.nn.softplus(-lam_ref[d])
            a_scr[pl.ds(row0 + r0, chunk), :] = jnp.exp(log_a)
            b_scr[pl.ds(row0 + r0, chunk), :] = jnp.sqrt(1.0 - jnp.exp(2.0 * log_a)) * (ig * xcv)
            return 0

        lax.fori_loop(0, n // chunk, rows, 0)

    def scan(row0, n, h, d):
        nb = n // 8

        def blk(bi, h):
            bidx = bi if d == 0 else nb - 1 - bi
            r0 = pl.multiple_of(row0 + bidx * 8, 8)
            ab = a_scr[pl.ds(r0, 8), :]
            bb = b_scr[pl.ds(r0, 8), :]
            out = jnp.zeros((8, GROUP_W), F32)
            for s in (range(8) if d == 0 else range(7, -1, -1)):
                hn = ab * h + bb
                h = jnp.broadcast_to(hn[s:s + 1, :], (8, GROUP_W))
                out = jnp.where(sub == s, hn, out)
            if d == 0:
                hs_scr[pl.ds(r0, 8), :] = out
            else:
                hs_scr[pl.ds(r0, 8), :] = hs_scr[pl.ds(r0, 8), :] + out
            return h

        return lax.fori_loop(0, nb, blk, h)

    @pl.when(j == 0)
    def _():
        for d in range(2):
            coefficients(recc_ref, ctx_len, 0, d)
            coefficients(recl_ref, seq, ctx_len, d)
            h = jnp.zeros((8, GROUP_W), F32)
            h = scan(0, ctx_len, h, d)
            scan(ctx_len, seq, h, d)

        def finish(gate_ref, row0, n):
            gte = gate_ref[...].astype(F32)
            y = jax.nn.gelu(gte) * hs_scr[row0:row0 + n, :]
            inv = lax.rsqrt(jnp.mean(y * y, axis=-1, keepdims=True) + EPS)
            y_scr[row0:row0 + n, :] = (y * inv * ng_ref[...]).astype(y_scr.dtype)

        finish(gatel_ref, ctx_len, seq)
        if need_ctx:
            finish(gatec_ref, 0, ctx_len)

    tl = o_ref.shape[0]
    src = jnp.where(j < nl, ctx_len + j * tl, 0)
    o_ref[...] = y_scr[pl.ds(pl.multiple_of(src, tl), tl), :]


def _rglru(u, conv_w, conv_b, wa_bd, ba, wx_bd, bx, lam, norm_g, *, bsz, seq, ctx_len, need_ctx):
    tl = ctx_len
    nl = seq // tl
    n_rows = bsz * seq + (bsz * ctx_len if need_ctx else 0)

    def orow(b, j):
        return jnp.where(j < nl, b * nl + j, bsz * nl + b)

    def full(shape):
        return pl.BlockSpec(shape, lambda b, j: (0,) * len(shape))

    body = functools.partial(_lru_body, nl=nl, seq=seq, ctx_len=ctx_len, need_ctx=need_ctx)
    n = ctx_len + seq
    return pl.pallas_call(
        body,
        grid=(bsz, nl + (1 if need_ctx else 0)),
        in_specs=[
            pl.BlockSpec((seq, GROUP_W), lambda b, j: (b, COL_REC)),
            pl.BlockSpec((seq, GROUP_W), lambda b, j: (b, COL_GD)),
            pl.BlockSpec((ctx_len, GROUP_W), lambda b, j: (bsz * nl + b, COL_REC)),
            pl.BlockSpec((ctx_len, GROUP_W), lambda b, j: (bsz * nl + b, COL_GD)),
            full((2, LRU_CONV_WIDTH, GROUP_W)), full((2, 1, GROUP_W)),
            full((2, GROUP_W, GROUP_W)), full((2, 1, GROUP_W)),
            full((2, GROUP_W, GROUP_W)), full((2, 1, GROUP_W)),
            full((2, 1, GROUP_W)), full((1, GROUP_W)),
        ],
        out_specs=pl.BlockSpec((tl, GROUP_W), lambda b, j: (orow(b, j), 0)),
        out_shape=jax.ShapeDtypeStruct((n_rows, GROUP_W), BF16),
        scratch_shapes=[pltpu.VMEM((seq + 16, GROUP_W), F32),
                        pltpu.VMEM((n, GROUP_W), F32), pltpu.VMEM((n, GROUP_W), F32),
                        pltpu.VMEM((n, GROUP_W), F32), pltpu.VMEM((n, GROUP_W), BF16)],
        compiler_params=_params(("arbitrary", "arbitrary")),
        name="rglru",
    )(u, u, u, u, conv_w, conv_b.reshape(2, 1, -1), wa_bd, ba.reshape(2, 1, -1),
      wx_bd, bx.reshape(2, 1, -1), lam.reshape(2, 1, -1), norm_g.reshape(1, -1))


def _pack_rows(h, o_ref):
    tm, d = h.shape
    half = d // 2
    words = half // LANES

    def bits(v):
        return pltpu.bitcast(v.astype(BF16).astype(F32), U32)

    for c in range(words):
        lo = bits(h[:, c * LANES:(c + 1) * LANES]) >> 16
        hi = bits(h[:, half + c * LANES:half + (c + 1) * LANES]) & jnp.uint32(0xFFFF0000)
        o_ref[pl.ds(c, tm, stride=words), :] = hi | lo


def _outproj_body(ya_ref, yb_ref, ys_ref, yd_ref, w_ref, x_ref, g1_ref, sc_ref, sh_ref, ng_ref,
                  rw_ref, rb_ref, xo_ref, hp_ref, eid_ref, gate_ref, rank_ref, cnt_ref, cnt_scr):
    i = pl.program_id(0)
    tm = x_ref.shape[0]

    @pl.when(i == 0)
    def _():
        cnt_scr[...] = jnp.zeros_like(cnt_scr)

    acc = jnp.dot(ya_ref[...], w_ref[0:GROUP_W, :], preferred_element_type=F32)
    acc = acc + jnp.dot(yb_ref[...], w_ref[GROUP_W:2 * GROUP_W, :], preferred_element_type=F32)
    acc = acc + jnp.dot(ys_ref[...], w_ref[2 * GROUP_W:3 * GROUP_W, :], preferred_element_type=F32)
    acc = acc + jnp.dot(yd_ref[...], w_ref[3 * GROUP_W:4 * GROUP_W, :], preferred_element_type=F32)
    x = x_ref[...] + g1_ref[0] * acc
    xo_ref[...] = x
    inv = lax.rsqrt(jnp.mean(x * x, axis=-1, keepdims=True) + EPS)
    h2 = (x * inv * ng_ref[...]) * (1.0 + sc_ref[0]) + sh_ref[0]
    _pack_rows(h2, hp_ref)

    logits = jnp.dot(h2, rw_ref[...], preferred_element_type=F32,
                     precision=lax.Precision.HIGHEST) + rb_ref[...]
    lane = lax.broadcasted_iota(I32, (tm, N_EXPERTS), 1).astype(F32)
    l = logits
    vals, sels = [], []
    for k in range(TOP_K):
        m = jnp.max(l, axis=-1, keepdims=True)
        idx = jnp.min(jnp.where(l == m, lane, float(N_EXPERTS)), axis=-1, keepdims=True)
        sel = lane == idx
        vals.append(m)
        sels.append(sel)
        eid_ref[:, k:k + 1] = idx.astype(I32)
        l = jnp.where(sel, -jnp.inf, l)
    es = [jnp.exp(v - vals[0]) for v in vals]
    den = es[0] + es[1] + es[2] + es[3]
    for k in range(TOP_K):
        gate_ref[:, k:k + 1] = es[k] / den

    onehots = [s.astype(F32) for s in sels]
    m_all = onehots[0] + onehots[1] + onehots[2] + onehots[3]
    r_i = lax.broadcasted_iota(I32, (tm, tm), 0)
    c_i = lax.broadcasted_iota(I32, (tm, tm), 1)
    tri = (c_i < r_i).astype(BF16)
    before = jnp.dot(tri, m_all.astype(BF16), preferred_element_type=F32) + cnt_scr[...]
    for k in range(TOP_K):
        rank_ref[:, k:k + 1] = jnp.sum(onehots[k] * before, axis=-1, keepdims=True).astype(I32)
    cnt_scr[...] = cnt_scr[...] + jnp.sum(m_all, axis=0, keepdims=True)
    cnt_ref[...] = cnt_scr[...]


def _outproj(ya, yb, ys, yd, w_out, x, mod3, norm_g, router_w, router_b, *,
             n_rows, n_lat_rows, seq, bsz, tm):
    d = x.shape[1]
    nlt = n_lat_rows // tm
    per_seq = seq // tm
    words = d // 2 // LANES

    def mrow(i):
        return jnp.where(i < nlt, i // per_seq, bsz)

    def yspec():
        return pl.BlockSpec((tm, GROUP_W), lambda i: (i, 0))

    def mspec(chunk):
        return pl.BlockSpec((1, 1, d), lambda i: (mrow(i), 0, chunk))

    def kspec():
        return pl.BlockSpec((tm, TOP_K), lambda i: (i, 0))

    return pl.pallas_call(
        _outproj_body,
        grid=(n_rows // tm,),
        in_specs=[yspec(), yspec(), yspec(), yspec(),
                  pl.BlockSpec((4 * GROUP_W, d), lambda i: (0, 0)),
                  pl.BlockSpec((tm, d), lambda i: (i, 0)),
                  mspec(2), mspec(4), mspec(3),
                  pl.BlockSpec((1, d), lambda i: (0, 0)),
                  pl.BlockSpec((d, N_EXPERTS), lambda i: (0, 0)),
                  pl.BlockSpec((1, N_EXPERTS), lambda i: (0, 0))],
        out_specs=[pl.BlockSpec((tm, d), lambda i: (i, 0)),
                   pl.BlockSpec((tm * words, LANES), lambda i: (i, 0)),
                   kspec(), kspec(), kspec(),
                   pl.BlockSpec((1, N_EXPERTS), lambda i: (0, 0))],
        out_shape=[jax.ShapeDtypeStruct((n_rows, d), F32),
                   jax.ShapeDtypeStruct((n_rows * words, LANES), U32),
                   jax.ShapeDtypeStruct((n_rows, TOP_K), I32),
                   jax.ShapeDtypeStruct((n_rows, TOP_K), F32),
                   jax.ShapeDtypeStruct((n_rows, TOP_K), I32),
                   jax.ShapeDtypeStruct((1, N_EXPERTS), F32)],
        scratch_shapes=[pltpu.VMEM((1, N_EXPERTS), F32)],
        compiler_params=_params(("arbitrary",)),
        name="outproj_route",
    )(ya, yb, ys, yd, w_out, x, mod3, mod3, mod3, norm_g.reshape(1, d), router_w,
      router_b.reshape(1, N_EXPERTS))


def _dispatch_body(dest_ref, hp_ref, xs_in_hbm, xs_hbm, sem, *, tm, words):
    del xs_in_hbm

    def row_copy(r, dst):
        return pltpu.make_async_copy(
            hp_ref.at[pl.ds(pl.multiple_of(r * words, words), words), :],
            xs_hbm.at[pl.ds(pl.multiple_of(dst * words, words), words), :], sem)

    def issue(r, _):
        for k in range(TOP_K):
            row_copy(r, dest_ref[r * TOP_K + k]).start()
        return 0

    lax.fori_loop(0, tm, issue, 0)

    def drain(r, _):
        for k in range(TOP_K):
            row_copy(0, 0).wait()
        return 0

    lax.fori_loop(0, tm, drain, 0)


def _dispatch(dest, hp, xs_init, *, n_tok, tm, words):
    body = functools.partial(_dispatch_body, tm=tm, words=words)
    return pl.pallas_call(
        body,
        grid=(n_tok // tm,),
        in_specs=[pl.BlockSpec((tm * TOP_K,), lambda i: (i,), memory_space=pltpu.SMEM),
                  pl.BlockSpec((tm * words, LANES), lambda i: (i, 0)),
                  pl.BlockSpec(memory_space=pl.ANY)],
        out_specs=pl.BlockSpec(memory_space=pl.ANY),
        out_shape=jax.ShapeDtypeStruct(xs_init.shape, xs_init.dtype),
        scratch_shapes=[pltpu.SemaphoreType.DMA(())],
        input_output_aliases={2: 0},
        compiler_params=_params(("arbitrary",)),
        name="moe_dispatch",
    )(dest, hp, xs_init)


def _w1prep_body(w_ref, s_ref, og_ref, ou_ref):
    for cb in range(w_ref.shape[1] // (2 * LANES)):
        blk = w_ref[:, cb * 2 * LANES:(cb + 1) * 2 * LANES].astype(BF16)
        r = jnp.dot(blk, s_ref[...], preferred_element_type=F32)
        og_ref[:, cb * LANES:(cb + 1) * LANES] = r[:, :LANES].astype(BF16)
        ou_ref[:, cb * LANES:(cb + 1) * LANES] = r[:, LANES:].astype(BF16)


def _w1prep(w1_all, layer):
    _, n_exp, d, f2 = w1_all.shape
    td = 512
    rows = jnp.arange(2 * LANES)
    target = jnp.where(rows % 2 == 0, rows // 2, LANES + rows // 2)
    sel = (target[:, None] == jnp.arange(2 * LANES)[None, :]).astype(BF16)
    out = jax.ShapeDtypeStruct((n_exp, d, f2 // 2), BF16)
    return pl.pallas_call(
        _w1prep_body,
        grid=(n_exp, d // td),
        in_specs=[pl.BlockSpec((None, None, td, f2), lambda e, i: (layer, e, i, 0)),
                  pl.BlockSpec((2 * LANES, 2 * LANES), lambda e, i: (0, 0))],
        out_specs=[pl.BlockSpec((None, td, f2 // 2), lambda e, i: (e, i, 0)),
                   pl.BlockSpec((None, td, f2 // 2), lambda e, i: (e, i, 0))],
        out_shape=[out, out],
        compiler_params=_params(("arbitrary", "arbitrary")),
        name="expert_w1_prep",
    )(w1_all, sel)


def _ffn_body(te_ref, nu_ref, xs_ref, w1g_ref, w1u_ref, b1g_ref, b1u_ref, w2_ref, b2_ref,
              ys_ref, x_scr, *, tm, d):
    r = pl.program_id(0)
    half = d // 2
    words = half // LANES
    out_words = d // LANES

    @pl.when(r < nu_ref[0])
    def _():
        for c in range(words):
            w = xs_ref[pl.ds(c, tm, stride=words), :]
            lo = pltpu.bitcast(w << 16, F32)
            hi = pltpu.bitcast(w & jnp.uint32(0xFFFF0000), F32)
            x_scr[:, c * LANES:(c + 1) * LANES] = lo.astype(BF16)
            x_scr[:, half + c * LANES:half + (c + 1) * LANES] = hi.astype(BF16)
        x = x_scr[...]
        g = jnp.dot(x, w1g_ref[...], preferred_element_type=F32) + b1g_ref[...]
        up = jnp.dot(x, w1u_ref[...], preferred_element_type=F32) + b1u_ref[...]
        g = jnp.minimum(g, SWIGLU_LIMIT)
        up = jnp.clip(up, -SWIGLU_LIMIT, SWIGLU_LIMIT)
        act = (up + 1.0) * (g * _sigmoid(SWIGLU_ALPHA * g))
        y = jnp.dot(act.astype(BF16), w2_ref[...], preferred_element_type=F32) + b2_ref[...]
        for c in range(out_words):
            ys_ref[pl.ds(c, tm, stride=out_words), :] = y[:, c * LANES:(c + 1) * LANES]

    @pl.when(r >= nu_ref[0])
    def _():
        ys_ref[...] = jnp.zeros_like(ys_ref)


def _expert_ffn(tile_expert, n_used, xs, w1g, w1u, b1g, b1u, w2, b2, *, n_tiles, tm, d):
    f = w1g.shape[2]
    words = d // 2 // LANES
    out_words = d // LANES

    def row(r, te, nu):
        return (jnp.minimum(r, nu[0] - 1), 0)

    def wspec(shape):
        return pl.BlockSpec((None,) + shape, lambda r, te, nu: (te[r], 0, 0))

    body = functools.partial(_ffn_body, tm=tm, d=d)
    return pl.pallas_call(
        body,
        grid_spec=pltpu.PrefetchScalarGridSpec(
            num_scalar_prefetch=2,
            grid=(n_tiles,),
            in_specs=[pl.BlockSpec((tm * words, LANES), row),
                      wspec((d, f)), wspec((d, f)), wspec((1, f)), wspec((1, f)),
                      wspec((f, d)), wspec((1, d))],
            out_specs=pl.BlockSpec((tm * out_words, LANES), lambda r, te, nu: (r, 0)),
            scratch_shapes=[pltpu.VMEM((tm, d), BF16)]),
        out_shape=jax.ShapeDtypeStruct((n_tiles * tm * out_words, LANES), F32),
        compiler_params=_params(("arbitrary",)),
        name="moe_expert_ffn",
    )(tile_expert, n_used, xs, w1g, w1u, b1g, b1u, w2, b2)


def _combine_body(dest_ref, gate_ref, x_ref, g2_ref, fg_ref, ys_hbm, o_ref, buf, sem,
                  *, tc, d, final):
    out_words = d // LANES

    def row_copy(src, slot):
        return pltpu.make_async_copy(
            ys_hbm.at[pl.ds(pl.multiple_of(src * out_words, out_words), out_words), :],
            buf.at[pl.ds(pl.multiple_of(slot * out_words, out_words), out_words), :], sem)

    def issue(r, _):
        for k in range(TOP_K):
            row_copy(dest_ref[r * TOP_K + k], k * tc + r).start()
        return 0

    lax.fori_loop(0, tc, issue, 0)

    def drain(r, _):
        for k in range(TOP_K):
            row_copy(0, 0).wait()
        return 0

    lax.fori_loop(0, tc, drain, 0)

    gates = gate_ref[...]
    for c in range(out_words):
        f = jnp.zeros((tc, LANES), F32)
        for k in range(TOP_K):
            rows = buf[pl.ds(k * tc * out_words + c, tc, stride=out_words), :]
            f = f + gates[:, k:k + 1] * rows
        cols = slice(c * LANES, (c + 1) * LANES)
        o_ref[:, cols] = x_ref[:, cols] + g2_ref[0][:, cols] * f
    if final:
        x = o_ref[...]
        inv = lax.rsqrt(jnp.mean(x * x, axis=-1, keepdims=True) + EPS)
        o_ref[...] = x * inv * fg_ref[...]


def _combine(dest, gate, x, mod3, final_g, ys, *, n_tok, n_lat_rows, seq, bsz, tc, final):
    d = x.shape[1]
    nlt = n_lat_rows // tc
    per_seq = seq // tc
    out_words = d // LANES

    def mrow(i):
        return jnp.where(i < nlt, i // per_seq, bsz)

    body = functools.partial(_combine_body, tc=tc, d=d, final=final)
    return pl.pallas_call(
        body,
        grid=(n_tok // tc,),
        in_specs=[pl.BlockSpec((tc * TOP_K,), lambda i: (i,), memory_space=pltpu.SMEM),
                  pl.BlockSpec((tc, TOP_K), lambda i: (i, 0)),
                  pl.BlockSpec((tc, d), lambda i: (i, 0)),
                  pl.BlockSpec((1, 1, d), lambda i: (mrow(i), 0, 5)),
                  pl.BlockSpec((1, d), lambda i: (0, 0)),
                  pl.BlockSpec(memory_space=pl.ANY)],
        out_specs=pl.BlockSpec((tc, d), lambda i: (i, 0)),
        out_shape=jax.ShapeDtypeStruct((n_tok, d), F32),
        scratch_shapes=[pltpu.VMEM((TOP_K * tc * out_words, LANES), F32),
                        pltpu.SemaphoreType.DMA(())],
        compiler_params=_params(("arbitrary",)),
        name="moe_combine",
    )(dest, gate, x, mod3, final_g.reshape(1, d), ys)


def _rope_tables(seq):
    rows = seq // GRID_W
    row = jnp.repeat(jnp.arange(rows, dtype=F32), GRID_W)
    col = jnp.tile(jnp.arange(GRID_W, dtype=F32), rows)
    half = HEAD_DIM // 2
    inv = ROPE_BASE ** (-jnp.arange(0, half, 2, dtype=F32) / half)
    ar = row[:, None] * inv
    ac = col[:, None] * inv
    ang = jnp.concatenate([ar, ar, ac, ac], axis=-1)
    ang = jnp.concatenate([ang, ang], axis=-1)
    low = (jnp.arange(LANES) % 32) < 16
    return jnp.cos(ang), jnp.where(low, -jnp.sin(ang), jnp.sin(ang))


def _block_diag(w):
    nd, nb, hi, ho = w.shape
    eye = jnp.eye(nb, dtype=w.dtype)
    return jnp.einsum('dhij,hg->dhigj', w, eye).reshape(nd, nb * hi, nb * ho)


def _moe_plan(eid, rank, counts, *, tm, n_tiles):
    cnt = counts.reshape(N_EXPERTS).astype(I32)
    padded = ((cnt + tm - 1) // tm) * tm
    ends = jnp.cumsum(padded)
    base = ends - padded
    dest = (base[eid] + rank).reshape(-1).astype(I32)
    tile_expert = jnp.searchsorted(ends // tm, jnp.arange(n_tiles, dtype=I32), side='right')
    tile_expert = jnp.minimum(tile_expert, N_EXPERTS - 1).astype(I32)
    n_used = (ends[-1] // tm).reshape(1).astype(I32)
    return dest, tile_expert, n_used


def kernel(x, c, ctx, c_ctx, ada_w, ada_b, norm1_g, norm2_g, w_in, w_out, conv_a_w, conv_a_b, ln_a_g, ln_a_b, diff_lambda, diff_norm_g, conv_c_w, norm_c_g, lru_conv_w, lru_conv_b, lru_wa, lru_ba, lru_wx, lru_bx, lru_lam, norm_d_g, router_w, router_b, exp_w1, exp_b1, exp_w2, exp_b2, final_g):
    bsz, seq, d = x.shape
    ctx_len = ctx.shape[1]
    depth = ada_w.shape[0]
    n_lat = bsz * seq
    n_ctx = bsz * ctx_len
    t_all = n_lat + n_ctx
    assert bsz + 1 <= MOD_ROWS and seq % ctx_len == 0 and ctx_len % 128 == 0
    tm_in = min(1024, seq, n_ctx)
    assert seq % tm_in == 0 and n_ctx % tm_in == 0
    tr = ctx_len if ctx_len <= 256 else 256
    assert ctx_len % tr == 0
    tm_ffn = 256
    tc = 128
    words = d // 2 // LANES

    cc = jnp.zeros((MOD_ROWS, d), F32).at[:bsz].set(c).at[bsz].set(c_ctx)
    mod = _modulation(cc, ada_w, ada_b)
    cos2, sin2 = _rope_tables(seq)
    xa = jnp.concatenate([x.reshape(n_lat, d), ctx.reshape(n_ctx, d)], axis=0)

    for i in range(depth):
        need_ctx = i < depth - 1
        lam_init = 0.8 - 0.6 * math.exp(-0.3 * i)
        n_rows = t_all if need_ctx else n_lat
        mod3 = mod[i].reshape(MOD_ROWS, 1, 6 * d)

        u = _inproj(xa, mod3, norm1_g[i], w_in[i].astype(BF16),
                    n_lat_rows=n_lat, seq=seq, bsz=bsz, tm=tm_in)
        ya, ys = _convmix(u, conv_a_w[i], conv_a_b[i], ln_a_g[i], ln_a_b[i], conv_c_w[i], norm_c_g[i],
                          n_rows=n_rows, n_lat_rows=n_lat, seq=seq, ctx_len=ctx_len, tr=tr)
        yb = _attention(u, cos2, sin2, diff_lambda[i], diff_norm_g[i], bsz=bsz, seq=seq,
                        ctx_len=ctx_len, need_ctx=need_ctx, lam_init=lam_init)
        yd = _rglru(u, lru_conv_w[i], lru_conv_b[i], _block_diag(lru_wa[i]).astype(BF16), lru_ba[i],
                    _block_diag(lru_wx[i]).astype(BF16), lru_bx[i], lru_lam[i], norm_d_g[i],
                    bsz=bsz, seq=seq, ctx_len=ctx_len, need_ctx=need_ctx)
        xm, hp, eid, gate, rank, counts = _outproj(
            ya, yb, ys, yd, w_out[i].astype(BF16), xa, mod3, norm2_g[i], router_w[i], router_b[i],
            n_rows=n_rows, n_lat_rows=n_lat, seq=seq, bsz=bsz, tm=tr)

        n_tiles = (n_rows * TOP_K) // tm_ffn + N_EXPERTS
        dest, tile_expert, n_used = _moe_plan(eid, rank, counts, tm=tm_ffn, n_tiles=n_tiles)
        xs = _dispatch(dest, hp, jnp.zeros((n_tiles * tm_ffn * words, LANES), U32),
                       n_tok=n_rows, tm=tr, words=words)
        ff = exp_w1.shape[3] // 2
        w1g, w1u = _w1prep(exp_w1, i)
        ysort = _expert_ffn(
            tile_expert, n_used, xs, w1g, w1u,
            exp_b1[i][:, 0::2].reshape(N_EXPERTS, 1, ff), exp_b1[i][:, 1::2].reshape(N_EXPERTS, 1, ff),
            exp_w2[i].astype(BF16), exp_b2[i].reshape(N_EXPERTS, 1, d),
            n_tiles=n_tiles, tm=tm_ffn, d=d)
        xa = _combine(dest, gate, xm, mod3, final_g, ysort, n_tok=n_rows, n_lat_rows=n_lat,
                      seq=seq, bsz=bsz, tc=tc, final=not need_ctx)

    return xa[:n_lat].reshape(bsz, seq, d)
```

```python
import functools
import math

import jax
import jax.numpy as jnp
from jax import lax
from jax.experimental import pallas as pl
from jax.experimental.pallas import tpu as pltpu

F32 = jnp.float32
BF16 = jnp.bfloat16
I32 = jnp.int32
U32 = jnp.uint32

GRID_W = 64
GROUP_W = 512
HEAD_DIM = 64
DIFF_HEADS = 4
CONV_A_WIDTH = 31
CONV_C_WIDTH = 3
LRU_CONV_WIDTH = 4
LRU_C = 8.0
ROPE_BASE = 10000.0
N_EXPERTS = 32
TOP_K = 4
SWIGLU_LIMIT = 7.0
SWIGLU_ALPHA = 1.702
EPS = 1e-6

LANES = 128
HALO = 16
MOD_ROWS = 16
VMEM_LIMIT = 56 * 1024 * 1024

COL_VAL, COL_GATE, COL_Q, COL_K, COL_V, COL_BG, COL_CG, COL_VC, COL_GD, COL_REC = range(10)


def _params(sem, vmem=VMEM_LIMIT):
    return pltpu.CompilerParams(dimension_semantics=sem, vmem_limit_bytes=vmem)


def _sigmoid(x):
    return 1.0 / (1.0 + jnp.exp(-x))


def _mod_body(c_ref, w_ref, b_ref, o_ref):
    c = c_ref[...]
    s = (c * _sigmoid(c)).astype(BF16)
    o_ref[...] = jnp.dot(s, w_ref[...].astype(BF16), preferred_element_type=F32) + b_ref[...]


def _modulation(cc, ada_w, ada_b):
    depth, d, n = ada_w.shape
    tn = 1024
    return pl.pallas_call(
        _mod_body,
        grid=(depth, n // tn),
        in_specs=[
            pl.BlockSpec((MOD_ROWS, d), lambda l, j: (0, 0)),
            pl.BlockSpec((None, d, tn), lambda l, j: (l, 0, j)),
            pl.BlockSpec((None, 1, tn), lambda l, j: (l, 0, j)),
        ],
        out_specs=pl.BlockSpec((None, MOD_ROWS, tn), lambda l, j: (l, 0, j)),
        out_shape=jax.ShapeDtypeStruct((depth, MOD_ROWS, n), F32),
        compiler_params=_params(("arbitrary", "arbitrary")),
        name="modulation",
    )(cc, ada_w, ada_b.reshape(depth, 1, n))


def _inproj_body(x_ref, g_ref, sc_ref, sh_ref, w_ref, o_ref, h_scr, *, slab):
    j = pl.program_id(1)
    tm = x_ref.shape[0]

    @pl.when(j == 0)
    def _():
        g = g_ref[...]
        sc = 1.0 + sc_ref[0]
        sh = sh_ref[0]

        def rows(s, _):
            r0 = pl.multiple_of(s * slab, slab)
            x = x_ref[pl.ds(r0, slab), :]
            inv = lax.rsqrt(jnp.mean(x * x, axis=-1, keepdims=True) + EPS)
            h_scr[pl.ds(r0, slab), :] = ((x * inv * g) * sc + sh).astype(BF16)
            return 0

        lax.fori_loop(0, tm // slab, rows, 0)

    o_ref[...] = jnp.dot(h_scr[...], w_ref[...], preferred_element_type=F32).astype(o_ref.dtype)


def _inproj(x, mod3, norm_g, w_in, *, n_lat_rows, seq, bsz, tm):
    t_all, d = x.shape
    n_in = w_in.shape[1]
    tn = 1280
    nlt = n_lat_rows // tm
    per_seq = seq // tm

    def mrow(i):
        return jnp.where(i < nlt, i // per_seq, bsz)

    return pl.pallas_call(
        functools.partial(_inproj_body, slab=min(128, tm)),
        grid=(t_all // tm, n_in // tn),
        in_specs=[
            pl.BlockSpec((tm, d), lambda i, j: (i, 0)),
            pl.BlockSpec((1, d), lambda i, j: (0, 0)),
            pl.BlockSpec((1, 1, d), lambda i, j: (mrow(i), 0, 1)),
            pl.BlockSpec((1, 1, d), lambda i, j: (mrow(i), 0, 0)),
            pl.BlockSpec((d, tn), lambda i, j: (0, j)),
        ],
        out_specs=pl.BlockSpec((tm, tn), lambda i, j: (i, j)),
        out_shape=jax.ShapeDtypeStruct((t_all, n_in), BF16),
        scratch_shapes=[pltpu.VMEM((tm, d), BF16)],
        compiler_params=_params(("arbitrary", "arbitrary")),
        name="inproj",
    )(x, norm_g.reshape(1, d), mod3, mod3, w_in)


def _convmix_body(val_ref, gate_ref, pval_ref, pgate_ref, nval_ref, ngate_ref,
                  bg_ref, cg_ref, vc_ref, pcg_ref, pvc_ref, ncg_ref, nvc_ref,
                  caw_ref, cab_ref, lng_ref, lnb_ref, ccw_ref, ncg_g_ref,
                  ya_ref, ys_ref, zp_scr, mp_scr, *, nlt, per_lat, per_ctx, chunk):
    i = pl.program_id(0)
    tr = val_ref.shape[0]
    pos = jnp.where(i < nlt, i % per_lat, (i - nlt) % per_ctx)
    per = jnp.where(i < nlt, per_lat, per_ctx)
    keep_prev = (pos > 0).astype(F32)
    keep_next = (pos < per - 1).astype(F32)

    def glu(v_ref, g_ref):
        return v_ref[...].astype(F32) * _sigmoid(g_ref[...].astype(F32))

    zp_scr[0:HALO, :] = glu(pval_ref, pgate_ref) * keep_prev
    zp_scr[HALO:HALO + tr, :] = glu(val_ref, gate_ref)
    zp_scr[HALO + tr:, :] = glu(nval_ref, ngate_ref) * keep_next

    mp_scr[0:HALO, :] = pcg_ref[...].astype(F32) * pvc_ref[...].astype(F32) * keep_prev
    mp_scr[HALO:HALO + tr, :] = cg_ref[...].astype(F32) * vc_ref[...].astype(F32)
    mp_scr[HALO + tr:, :] = ncg_ref[...].astype(F32) * nvc_ref[...].astype(F32) * keep_next

    half_a = (CONV_A_WIDTH - 1) // 2
    half_c = (CONV_C_WIDTH - 1) // 2

    def rows(ci, _):
        r0 = pl.multiple_of(ci * chunk, chunk)
        blk = zp_scr[pl.ds(r0, chunk + 2 * HALO), :]
        acc = jnp.zeros((chunk, GROUP_W), F32)
        for k in range(CONV_A_WIDTH):
            off = HALO - half_a + k
            acc = acc + caw_ref[k:k + 1, :] * blk[off:off + chunk, :]
        z = acc + cab_ref[...]
        mu = jnp.mean(z, axis=-1, keepdims=True)
        zc = z - mu
        var = jnp.mean(zc * zc, axis=-1, keepdims=True)
        ln = zc * lax.rsqrt(var + EPS) * lng_ref[...] + lnb_ref[...]
        ya_ref[pl.ds(r0, chunk), :] = (ln * _sigmoid(ln)).astype(ya_ref.dtype)
        mblk = mp_scr[pl.ds(r0, chunk + 2 * HALO), :]
        cacc = jnp.zeros((chunk, GROUP_W), F32)
        for k in range(CONV_C_WIDTH):
            off = HALO - half_c + k
            cacc = cacc + ccw_ref[k:k + 1, :] * mblk[off:off + chunk, :]
        y = bg_ref[pl.ds(r0, chunk), :].astype(F32) * cacc
        inv = lax.rsqrt(jnp.mean(y * y, axis=-1, keepdims=True) + EPS)
        ys_ref[pl.ds(r0, chunk), :] = (y * inv * ncg_g_ref[...]).astype(ys_ref.dtype)
        return 0

    lax.fori_loop(0, tr // chunk, rows, 0)


def _convmix(u, conv_a_w, conv_a_b, ln_a_g, ln_a_b, conv_c_w, norm_c_g, *,
             n_rows, n_lat_rows, seq, ctx_len, tr):
    t_all = u.shape[0]
    nlt = n_lat_rows // tr
    hb = tr // HALO
    last_h = t_all // HALO - 1

    def cur(col):
        return pl.BlockSpec((tr, GROUP_W), lambda i: (i, col))

    def prev(col):
        return pl.BlockSpec((HALO, GROUP_W), lambda i: (jnp.maximum(i * hb - 1, 0), col))

    def nxt(col):
        return pl.BlockSpec((HALO, GROUP_W), lambda i: (jnp.minimum((i + 1) * hb, last_h), col))

    def small(r):
        return pl.BlockSpec((r, GROUP_W), lambda i: (0, 0))

    body = functools.partial(_convmix_body, nlt=nlt, per_lat=seq // tr,
                             per_ctx=max(ctx_len // tr, 1), chunk=64)
    return pl.pallas_call(
        body,
        grid=(n_rows // tr,),
        in_specs=[cur(COL_VAL), cur(COL_GATE), prev(COL_VAL), prev(COL_GATE), nxt(COL_VAL), nxt(COL_GATE),
                  cur(COL_BG), cur(COL_CG), cur(COL_VC), prev(COL_CG), prev(COL_VC), nxt(COL_CG), nxt(COL_VC),
                  small(CONV_A_WIDTH), small(1), small(1), small(1), small(CONV_C_WIDTH), small(1)],
        out_specs=[pl.BlockSpec((tr, GROUP_W), lambda i: (i, 0)),
                   pl.BlockSpec((tr, GROUP_W), lambda i: (i, 0))],
        out_shape=[jax.ShapeDtypeStruct((n_rows, GROUP_W), BF16),
                   jax.ShapeDtypeStruct((n_rows, GROUP_W), BF16)],
        scratch_shapes=[pltpu.VMEM((tr + 2 * HALO, GROUP_W), F32),
                        pltpu.VMEM((tr + 2 * HALO, GROUP_W), F32)],
        compiler_params=_params(("arbitrary",)),
        name="convmix",
    )(*([u] * 13), conv_a_w, conv_a_b.reshape(1, -1), ln_a_g.reshape(1, -1), ln_a_b.reshape(1, -1),
      conv_c_w, norm_c_g.reshape(1, -1))


def _rope(x, cos, sin_signed, low):
    rot = jnp.where(low, pltpu.roll(x, LANES - 16, axis=1), pltpu.roll(x, 16, axis=1))
    return x * cos + rot * sin_signed


def _attn_body(q_ref, kl_ref, vl_ref, kc_ref, vc_ref, cosq_ref, sinq_ref, cosk_ref, sink_ref,
               lam_ref, g_ref, o_ref, k_scr, v_scr, *, nq, ctx_len, lam_init):
    i = pl.program_id(2)
    tq = q_ref.shape[0]
    lane = lax.broadcasted_iota(I32, (1, LANES), 1)
    low = (lane % 32) < 16
    first = lane < HEAD_DIM

    @pl.when(i == 0)
    def _():
        k_scr[0:ctx_len, :] = kc_ref[...]
        v_scr[0:ctx_len, :] = vc_ref[...]
        kl = _rope(kl_ref[...].astype(F32), cosk_ref[...], sink_ref[...], low)
        k_scr[ctx_len:, :] = kl.astype(BF16)
        v_scr[ctx_len:, :] = vl_ref[...]

    lv = lam_ref[...]
    lam = (jnp.exp(jnp.sum(lv[0:1, :] * lv[1:2, :], axis=-1, keepdims=True))
           - jnp.exp(jnp.sum(lv[2:3, :] * lv[3:4, :], axis=-1, keepdims=True)) + lam_init)

    def attend(q, k, v):
        q = q * (HEAD_DIM ** -0.5)

        def unnormalised(qh):
            s = lax.dot_general(qh.astype(BF16), k, (((1,), (1,)), ((), ())),
                                preferred_element_type=F32)
            e = jnp.exp(s - jnp.max(s, axis=-1, keepdims=True))
            return e, 1.0 / jnp.sum(e, axis=-1, keepdims=True)

        e1, inv1 = unnormalised(jnp.where(first, q, 0.0))
        e2, inv2 = unnormalised(jnp.where(first, 0.0, q))
        w = (e1 * inv1 - e2 * (lam * inv2)).astype(BF16)
        o = jnp.dot(w, v, preferred_element_type=F32)
        inv = lax.rsqrt(jnp.mean(o * o, axis=-1, keepdims=True) + EPS)
        o_ref[...] = ((o * inv * g_ref[...]) * (1.0 - lam_init)).astype(o_ref.dtype)

    @pl.when(i < nq)
    def _():
        q = _rope(q_ref[...].astype(F32), cosq_ref[...], sinq_ref[...], low)
        attend(q, k_scr[...], v_scr[...])

    @pl.when(i >= nq)
    def _():
        attend(q_ref[...].astype(F32), k_scr[0:ctx_len, :], v_scr[0:ctx_len, :])


def _attention(u, cos2, sin2, diff_lambda, diff_norm_g, *, bsz, seq, ctx_len, need_ctx, lam_init):
    tq = ctx_len
    nq = seq // tq
    n_rows = bsz * seq + (bsz * ctx_len if need_ctx else 0)
    hq, hk, hv = (COL_Q * GROUP_W // LANES, COL_K * GROUP_W // LANES, COL_V * GROUP_W // LANES)

    def qrow(b, i):
        return jnp.where(i < nq, b * nq + i, bsz * nq + b)

    body = functools.partial(_attn_body, nq=nq, ctx_len=ctx_len, lam_init=lam_init)
    return pl.pallas_call(
        body,
        grid=(bsz, DIFF_HEADS, nq + (1 if need_ctx else 0)),
        in_specs=[
            pl.BlockSpec((tq, LANES), lambda b, h, i: (qrow(b, i), hq + h)),
            pl.BlockSpec((seq, LANES), lambda b, h, i: (b, hk + h)),
            pl.BlockSpec((seq, LANES), lambda b, h, i: (b, hv + h)),
            pl.BlockSpec((ctx_len, LANES), lambda b, h, i: (bsz * nq + b, hk + h)),
            pl.BlockSpec((ctx_len, LANES), lambda b, h, i: (bsz * nq + b, hv + h)),
            pl.BlockSpec((tq, LANES), lambda b, h, i: (jnp.minimum(i, nq - 1), 0)),
            pl.BlockSpec((tq, LANES), lambda b, h, i: (jnp.minimum(i, nq - 1), 0)),
            pl.BlockSpec((seq, LANES), lambda b, h, i: (0, 0)),
            pl.BlockSpec((seq, LANES), lambda b, h, i: (0, 0)),
            pl.BlockSpec((4, HEAD_DIM), lambda b, h, i: (0, 0)),
            pl.BlockSpec((1, LANES), lambda b, h, i: (0, 0)),
        ],
        out_specs=pl.BlockSpec((tq, LANES), lambda b, h, i: (qrow(b, i), h)),
        out_shape=jax.ShapeDtypeStruct((n_rows, GROUP_W), BF16),
        scratch_shapes=[pltpu.VMEM((ctx_len + seq, LANES), BF16),
                        pltpu.VMEM((ctx_len + seq, LANES), BF16)],
        compiler_params=_params(("arbitrary", "arbitrary", "arbitrary")),
        name="diff_attention",
    )(u, u, u, u, u, cos2, sin2, cos2, sin2, diff_lambda, diff_norm_g.reshape(1, -1))


def _lru_body(recl_ref, gatel_ref, recc_ref, gatec_ref, cw_ref, cb_ref, wa_ref, ba_ref,
              wx_ref, bx_ref, lam_ref, ng_ref, o_ref,
              xp_scr, a_scr, b_scr, hs_scr, y_scr, *, nl, seq, ctx_len, need_ctx):
    j = pl.program_id(1)
    pad = 8
    sub = lax.broadcasted_iota(I32, (8, GROUP_W), 0)

    def coefficients(rec_ref, n, row0, d):
        xp_scr[...] = jnp.zeros_like(xp_scr)
        xp_scr[pad:pad + n, :] = rec_ref[...].astype(F32)
        base = pad if d == 1 else pad - (LRU_CONV_WIDTH - 1)
        chunk = 128

        def rows(ci, _):
            r0 = pl.multiple_of(ci * chunk, chunk)
            blk = xp_scr[pl.ds(r0, chunk + 2 * pad), :]
            xcv = jnp.zeros((chunk, GROUP_W), F32)
            for k in range(LRU_CONV_WIDTH):
                xcv = xcv + cw_ref[d, k:k + 1, :] * blk[base + k:base + k + chunk, :]
            xcv = xcv + cb_ref[d]
            xb = xcv.astype(BF16)
            r = _sigmoid(jnp.dot(xb, wa_ref[d], preferred_element_type=F32) + ba_ref[d])
            ig = _sigmoid(jnp.dot(xb, wx_ref[d], preferred_element_type=F32) + bx_ref[d])
            log_a = -LRU_C * r * jax.nn.softplus(-lam_ref[d])
            a_scr[pl.ds(row0 + r0, chunk), :] = jnp.exp(log_a)
            b_scr[pl.ds(row0 + r0, chunk), :] = jnp.sqrt(1.0 - jnp.exp(2.0 * log_a)) * (ig * xcv)
            return 0

        lax.fori_loop(0, n // chunk, rows, 0)

    def scan(row0, n, h, d):
        nb = n // 8

        def blk(bi, h):
            bidx = bi if d == 0 else nb - 1 - bi
            r0 = pl.multiple_of(row0 + bidx * 8, 8)
            ab = a_scr[pl.ds(r0, 8), :]
            bb = b_scr[pl.ds(r0, 8), :]
            out = jnp.zeros((8, GROUP_W), F32)
            for s in (range(8) if d == 0 else range(7, -1, -1)):
                hn = ab * h + bb
                h = jnp.broadcast_to(hn[s:s + 1, :], (8, GROUP_W))
                out = jnp.where(sub == s, hn, out)
            if d == 0:
                hs_scr[pl.ds(r0, 8), :] = out
            else:
                hs_scr[pl.ds(r0, 8), :] = hs_scr[pl.ds(r0, 8), :] + out
            return h

        return lax.fori_loop(0, nb, blk, h)

    @pl.when(j == 0)
    def _():
        for d in range(2):
            coefficients(recc_ref, ctx_len, 0, d)
            coefficients(recl_ref, seq, ctx_len, d)
            h = jnp.zeros((8, GROUP_W), F32)
            h = scan(0, ctx_len, h, d)
            scan(ctx_len, seq, h, d)

        def finish(gate_ref, row0, n):
            gte = gate_ref[...].astype(F32)
            y = jax.nn.gelu(gte) * hs_scr[row0:row0 + n, :]
            inv = lax.rsqrt(jnp.mean(y * y, axis=-1, keepdims=True) + EPS)
            y_scr[row0:row0 + n, :] = (y * inv * ng_ref[...]).astype(y_scr.dtype)

        finish(gatel_ref, ctx_len, seq)
        if need_ctx:
            finish(gatec_ref, 0, ctx_len)

    tl = o_ref.shape[0]
    src = jnp.where(j < nl, ctx_len + j * tl, 0)
    o_ref[...] = y_scr[pl.ds(pl.multiple_of(src, tl), tl), :]


def _rglru(u, conv_w, conv_b, wa_bd, ba, wx_bd, bx, lam, norm_g, *, bsz, seq, ctx_len, need_ctx):
    tl = ctx_len
    nl = seq // tl
    n_rows = bsz * seq + (bsz * ctx_len if need_ctx else 0)

    def orow(b, j):
        return jnp.where(j < nl, b * nl + j, bsz * nl + b)

    def full(shape):
        return pl.BlockSpec(shape, lambda b, j: (0,) * len(shape))

    body = functools.partial(_lru_body, nl=nl, seq=seq, ctx_len=ctx_len, need_ctx=need_ctx)
    n = ctx_len + seq
    return pl.pallas_call(
        body,
        grid=(bsz, nl + (1 if need_ctx else 0)),
        in_specs=[
            pl.BlockSpec((seq, GROUP_W), lambda b, j: (b, COL_REC)),
            pl.BlockSpec((seq, GROUP_W), lambda b, j: (b, COL_GD)),
            pl.BlockSpec((ctx_len, GROUP_W), lambda b, j: (bsz * nl + b, COL_REC)),
            pl.BlockSpec((ctx_len, GROUP_W), lambda b, j: (bsz * nl + b, COL_GD)),
            full((2, LRU_CONV_WIDTH, GROUP_W)), full((2, 1, GROUP_W)),
            full((2, GROUP_W, GROUP_W)), full((2, 1, GROUP_W)),
            full((2, GROUP_W, GROUP_W)), full((2, 1, GROUP_W)),
            full((2, 1, GROUP_W)), full((1, GROUP_W)),
        ],
        out_specs=pl.BlockSpec((tl, GROUP_W), lambda b, j: (orow(b, j), 0)),
        out_shape=jax.ShapeDtypeStruct((n_rows, GROUP_W), BF16),
        scratch_shapes=[pltpu.VMEM((seq + 16, GROUP_W), F32),
                        pltpu.VMEM((n, GROUP_W), F32), pltpu.VMEM((n, GROUP_W), F32),
                        pltpu.VMEM((n, GROUP_W), F32), pltpu.VMEM((n, GROUP_W), BF16)],
        compiler_params=_params(("arbitrary", "arbitrary")),
        name="rglru",
    )(u, u, u, u, conv_w, conv_b.reshape(2, 1, -1), wa_bd, ba.reshape(2, 1, -1),
      wx_bd, bx.reshape(2, 1, -1), lam.reshape(2, 1, -1), norm_g.reshape(1, -1))


def _pack_rows(h, o_ref):
    tm, d = h.shape
    half = d // 2
    words = half // LANES

    def bits(v):
        return pltpu.bitcast(v.astype(BF16).astype(F32), U32)

    for c in range(words):
        lo = bits(h[:, c * LANES:(c + 1) * LANES]) >> 16
        hi = bits(h[:, half + c * LANES:half + (c + 1) * LANES]) & jnp.uint32(0xFFFF0000)
        o_ref[pl.ds(c, tm, stride=words), :] = hi | lo


def _outproj_body(ya_ref, yb_ref, ys_ref, yd_ref, w_ref, x_ref, g1_ref, sc_ref, sh_ref, ng_ref,
                  rw_ref, rb_ref, xo_ref, hp_ref, eid_ref, gate_ref, rank_ref, cnt_ref, cnt_scr):
    i = pl.program_id(0)
    tm = x_ref.shape[0]

    @pl.when(i == 0)
    def _():
        cnt_scr[...] = jnp.zeros_like(cnt_scr)

    acc = jnp.dot(ya_ref[...], w_ref[0:GROUP_W, :], preferred_element_type=F32)
    acc = acc + jnp.dot(yb_ref[...], w_ref[GROUP_W:2 * GROUP_W, :], preferred_element_type=F32)
    acc = acc + jnp.dot(ys_ref[...], w_ref[2 * GROUP_W:3 * GROUP_W, :], preferred_element_type=F32)
    acc = acc + jnp.dot(yd_ref[...], w_ref[3 * GROUP_W:4 * GROUP_W, :], preferred_element_type=F32)
    x = x_ref[...] + g1_ref[0] * acc
    xo_ref[...] = x
    inv = lax.rsqrt(jnp.mean(x * x, axis=-1, keepdims=True) + EPS)
    h2 = (x * inv * ng_ref[...]) * (1.0 + sc_ref[0]) + sh_ref[0]
    _pack_rows(h2, hp_ref)

    logits = jnp.dot(h2, rw_ref[...], preferred_element_type=F32,
                     precision=lax.Precision.HIGHEST) + rb_ref[...]
    lane = lax.broadcasted_iota(I32, (tm, N_EXPERTS), 1).astype(F32)
    l = logits
    vals, sels = [], []
    for k in range(TOP_K):
        m = jnp.max(l, axis=-1, keepdims=True)
        idx = jnp.min(jnp.where(l == m, lane, float(N_EXPERTS)), axis=-1, keepdims=True)
        sel = lane == idx
        vals.append(m)
        sels.append(sel)
        eid_ref[:, k:k + 1] = idx.astype(I32)
        l = jnp.where(sel, -jnp.inf, l)
    es = [jnp.exp(v - vals[0]) for v in vals]
    den = es[0] + es[1] + es[2] + es[3]
    for k in range(TOP_K):
        gate_ref[:, k:k + 1] = es[k] / den

    onehots = [s.astype(F32) for s in sels]
    m_all = onehots[0] + onehots[1] + onehots[2] + onehots[3]
    r_i = lax.broadcasted_iota(I32, (tm, tm), 0)
    c_i = lax.broadcasted_iota(I32, (tm, tm), 1)
    tri = (c_i < r_i).astype(BF16)
    before = jnp.dot(tri, m_all.astype(BF16), preferred_element_type=F32) + cnt_scr[...]
    for k in range(TOP_K):
        rank_ref[:, k:k + 1] = jnp.sum(onehots[k] * before, axis=-1, keepdims=True).astype(I32)
    cnt_scr[...] = cnt_scr[...] + jnp.sum(m_all, axis=0, keepdims=True)
    cnt_ref[...] = cnt_scr[...]


def _outproj(ya, yb, ys, yd, w_out, x, mod3, norm_g, router_w, router_b, *,
             n_rows, n_lat_rows, seq, bsz, tm):
    d = x.shape[1]
    nlt = n_lat_rows // tm
    per_seq = seq // tm
    words = d // 2 // LANES

    def mrow(i):
        return jnp.where(i < nlt, i // per_seq, bsz)

    def yspec():
        return pl.BlockSpec((tm, GROUP_W), lambda i: (i, 0))

    def mspec(chunk):
        return pl.BlockSpec((1, 1, d), lambda i: (mrow(i), 0, chunk))

    def kspec():
        return pl.BlockSpec((tm, TOP_K), lambda i: (i, 0))

    return pl.pallas_call(
        _outproj_body,
        grid=(n_rows // tm,),
        in_specs=[yspec(), yspec(), yspec(), yspec(),
                  pl.BlockSpec((4 * GROUP_W, d), lambda i: (0, 0)),
                  pl.BlockSpec((tm, d), lambda i: (i, 0)),
                  mspec(2), mspec(4), mspec(3),
                  pl.BlockSpec((1, d), lambda i: (0, 0)),
                  pl.BlockSpec((d, N_EXPERTS), lambda i: (0, 0)),
                  pl.BlockSpec((1, N_EXPERTS), lambda i: (0, 0))],
        out_specs=[pl.BlockSpec((tm, d), lambda i: (i, 0)),
                   pl.BlockSpec((tm * words, LANES), lambda i: (i, 0)),
                   kspec(), kspec(), kspec(),
                   pl.BlockSpec((1, N_EXPERTS), lambda i: (0, 0))],
        out_shape=[jax.ShapeDtypeStruct((n_rows, d), F32),
                   jax.ShapeDtypeStruct((n_rows * words, LANES), U32),
                   jax.ShapeDtypeStruct((n_rows, TOP_K), I32),
                   jax.ShapeDtypeStruct((n_rows, TOP_K), F32),
                   jax.ShapeDtypeStruct((n_rows, TOP_K), I32),
                   jax.ShapeDtypeStruct((1, N_EXPERTS), F32)],
        scratch_shapes=[pltpu.VMEM((1, N_EXPERTS), F32)],
        compiler_params=_params(("arbitrary",)),
        name="outproj_route",
    )(ya, yb, ys, yd, w_out, x, mod3, mod3, mod3, norm_g.reshape(1, d), router_w,
      router_b.reshape(1, N_EXPERTS))


def _dispatch_body(dest_ref, hp_ref, xs_in_hbm, xs_hbm, sem, *, tm, words):
    del xs_in_hbm

    def row_copy(r, dst):
        return pltpu.make_async_copy(
            hp_ref.at[pl.ds(pl.multiple_of(r * words, words), words), :],
            xs_hbm.at[pl.ds(pl.multiple_of(dst * words, words), words), :], sem)

    def issue(r, _):
        for k in range(TOP_K):
            row_copy(r, dest_ref[r * TOP_K + k]).start()
        return 0

    lax.fori_loop(0, tm, issue, 0)

    def drain(r, _):
        for k in range(TOP_K):
            row_copy(0, 0).wait()
        return 0

    lax.fori_loop(0, tm, drain, 0)


def _dispatch(dest, hp, xs_init, *, n_tok, tm, words):
    body = functools.partial(_dispatch_body, tm=tm, words=words)
    return pl.pallas_call(
        body,
        grid=(n_tok // tm,),
        in_specs=[pl.BlockSpec((tm * TOP_K,), lambda i: (i,), memory_space=pltpu.SMEM),
                  pl.BlockSpec((tm * words, LANES), lambda i: (i, 0)),
                  pl.BlockSpec(memory_space=pl.ANY)],
        out_specs=pl.BlockSpec(memory_space=pl.ANY),
        out_shape=jax.ShapeDtypeStruct(xs_init.shape, xs_init.dtype),
        scratch_shapes=[pltpu.SemaphoreType.DMA(())],
        input_output_aliases={2: 0},
        compiler_params=_params(("arbitrary",)),
        name="moe_dispatch",
    )(dest, hp, xs_init)


def _w1prep_body(w_ref, s_ref, og_ref, ou_ref):
    for cb in range(w_ref.shape[1] // (2 * LANES)):
        blk = w_ref[:, cb * 2 * LANES:(cb + 1) * 2 * LANES].astype(BF16)
        r = jnp.dot(blk, s_ref[...], preferred_element_type=F32)
        og_ref[:, cb * LANES:(cb + 1) * LANES] = r[:, :LANES].astype(BF16)
        ou_ref[:, cb * LANES:(cb + 1) * LANES] = r[:, LANES:].astype(BF16)


def _w1prep(w1_all, layer):
    _, n_exp, d, f2 = w1_all.shape
    td = 512
    rows = jnp.arange(2 * LANES)
    target = jnp.where(rows % 2 == 0, rows // 2, LANES + rows // 2)
    sel = (target[:, None] == jnp.arange(2 * LANES)[None, :]).astype(BF16)
    out = jax.ShapeDtypeStruct((n_exp, d, f2 // 2), BF16)
    return pl.pallas_call(
        _w1prep_body,
        grid=(n_exp, d // td),
        in_specs=[pl.BlockSpec((None, None, td, f2), lambda e, i: (layer, e, i, 0)),
                  pl.BlockSpec((2 * LANES, 2 * LANES), lambda e, i: (0, 0))],
        out_specs=[pl.BlockSpec((None, td, f2 // 2), lambda e, i: (e, i, 0)),
                   pl.BlockSpec((None, td, f2 // 2), lambda e, i: (e, i, 0))],
        out_shape=[out, out],
        compiler_params=_params(("arbitrary", "arbitrary")),
        name="expert_w1_prep",
    )(w1_all, sel)


def _ffn_body(te_ref, nu_ref, xs_ref, w1g_ref, w1u_ref, b1g_ref, b1u_ref, w2_ref, b2_ref,
              ys_ref, x_scr, *, tm, d):
    r = pl.program_id(0)
    half = d // 2
    words = half // LANES
    out_words = d // LANES

    @pl.when(r < nu_ref[0])
    def _():
        for c in range(words):
            w = xs_ref[pl.ds(c, tm, stride=words), :]
            lo = pltpu.bitcast(w << 16, F32)
            hi = pltpu.bitcast(w & jnp.uint32(0xFFFF0000), F32)
            x_scr[:, c * LANES:(c + 1) * LANES] = lo.astype(BF16)
            x_scr[:, half + c * LANES:half + (c + 1) * LANES] = hi.astype(BF16)
        x = x_scr[...]
        g = jnp.dot(x, w1g_ref[...], preferred_element_type=F32) + b1g_ref[...]
        up = jnp.dot(x, w1u_ref[...], preferred_element_type=F32) + b1u_ref[...]
        g = jnp.minimum(g, SWIGLU_LIMIT)
        up = jnp.clip(up, -SWIGLU_LIMIT, SWIGLU_LIMIT)
        act = (up + 1.0) * (g * _sigmoid(SWIGLU_ALPHA * g))
        y = jnp.dot(act.astype(BF16), w2_ref[...], preferred_element_type=F32) + b2_ref[...]
        for c in range(out_words):
            ys_ref[pl.ds(c, tm, stride=out_words), :] = y[:, c * LANES:(c + 1) * LANES]

    @pl.when(r >= nu_ref[0])
    def _():
        ys_ref[...] = jnp.zeros_like(ys_ref)


def _expert_ffn(tile_expert, n_used, xs, w1g, w1u, b1g, b1u, w2, b2, *, n_tiles, tm, d):
    f = w1g.shape[2]
    words = d // 2 // LANES
    out_words = d // LANES

    def row(r, te, nu):
        return (jnp.minimum(r, nu[0] - 1), 0)

    def wspec(shape):
        return pl.BlockSpec((None,) + shape, lambda r, te, nu: (te[r], 0, 0))

    body = functools.partial(_ffn_body, tm=tm, d=d)
    return pl.pallas_call(
        body,
        grid_spec=pltpu.PrefetchScalarGridSpec(
            num_scalar_prefetch=2,
            grid=(n_tiles,),
            in_specs=[pl.BlockSpec((tm * words, LANES), row),
                      wspec((d, f)), wspec((d, f)), wspec((1, f)), wspec((1, f)),
                      wspec((f, d)), wspec((1, d))],
            out_specs=pl.BlockSpec((tm * out_words, LANES), lambda r, te, nu: (r, 0)),
            scratch_shapes=[pltpu.VMEM((tm, d), BF16)]),
        out_shape=jax.ShapeDtypeStruct((n_tiles * tm * out_words, LANES), F32),
        compiler_params=_params(("arbitrary",)),
        name="moe_expert_ffn",
    )(tile_expert, n_used, xs, w1g, w1u, b1g, b1u, w2, b2)


def _combine_body(dest_ref, gate_ref, x_ref, g2_ref, fg_ref, ys_hbm, o_ref, buf, sem,
                  *, tc, d, final):
    out_words = d // LANES

    def row_copy(src, slot):
        return pltpu.make_async_copy(
            ys_hbm.at[pl.ds(pl.multiple_of(src * out_words, out_words), out_words), :],
            buf.at[pl.ds(pl.multiple_of(slot * out_words, out_words), out_words), :], sem)

    def issue(r, _):
        for k in range(TOP_K):
            row_copy(dest_ref[r * TOP_K + k], k * tc + r).start()
        return 0

    lax.fori_loop(0, tc, issue, 0)

    def drain(r, _):
        for k in range(TOP_K):
            row_copy(0, 0).wait()
        return 0

    lax.fori_loop(0, tc, drain, 0)

    gates = gate_ref[...]
    for c in range(out_words):
        f = jnp.zeros((tc, LANES), F32)
        for k in range(TOP_K):
            rows = buf[pl.ds(k * tc * out_words + c, tc, stride=out_words), :]
            f = f + gates[:, k:k + 1] * rows
        cols = slice(c * LANES, (c + 1) * LANES)
        o_ref[:, cols] = x_ref[:, cols] + g2_ref[0][:, cols] * f
    if final:
        x = o_ref[...]
        inv = lax.rsqrt(jnp.mean(x * x, axis=-1, keepdims=True) + EPS)
        o_ref[...] = x * inv * fg_ref[...]


def _combine(dest, gate, x, mod3, final_g, ys, *, n_tok, n_lat_rows, seq, bsz, tc, final):
    d = x.shape[1]
    nlt = n_lat_rows // tc
    per_seq = seq // tc
    out_words = d // LANES

    def mrow(i):
        return jnp.where(i < nlt, i // per_seq, bsz)

    body = functools.partial(_combine_body, tc=tc, d=d, final=final)
    return pl.pallas_call(
        body,
        grid=(n_tok // tc,),
        in_specs=[pl.BlockSpec((tc * TOP_K,), lambda i: (i,), memory_space=pltpu.SMEM),
                  pl.BlockSpec((tc, TOP_K), lambda i: (i, 0)),
                  pl.BlockSpec((tc, d), lambda i: (i, 0)),
                  pl.BlockSpec((1, 1, d), lambda i: (mrow(i), 0, 5)),
                  pl.BlockSpec((1, d), lambda i: (0, 0)),
                  pl.BlockSpec(memory_space=pl.ANY)],
        out_specs=pl.BlockSpec((tc, d), lambda i: (i, 0)),
        out_shape=jax.ShapeDtypeStruct((n_tok, d), F32),
        scratch_shapes=[pltpu.VMEM((TOP_K * tc * out_words, LANES), F32),
                        pltpu.SemaphoreType.DMA(())],
        compiler_params=_params(("arbitrary",)),
        name="moe_combine",
    )(dest, gate, x, mod3, final_g.reshape(1, d), ys)


def _rope_tables(seq):
    rows = seq // GRID_W
    row = jnp.repeat(jnp.arange(rows, dtype=F32), GRID_W)
    col = jnp.tile(jnp.arange(GRID_W, dtype=F32), rows)
    half = HEAD_DIM // 2
    inv = ROPE_BASE ** (-jnp.arange(0, half, 2, dtype=F32) / half)
    ar = row[:, None] * inv
    ac = col[:, None] * inv
    ang = jnp.concatenate([ar, ar, ac, ac], axis=-1)
    ang = jnp.concatenate([ang, ang], axis=-1)
    low = (jnp.arange(LANES) % 32) < 16
    return jnp.cos(ang), jnp.where(low, -jnp.sin(ang), jnp.sin(ang))


def _block_diag(w):
    nd, nb, hi, ho = w.shape
    eye = jnp.eye(nb, dtype=w.dtype)
    return jnp.einsum('dhij,hg->dhigj', w, eye).reshape(nd, nb * hi, nb * ho)


def _moe_plan(eid, rank, counts, *, tm, n_tiles):
    cnt = counts.reshape(N_EXPERTS).astype(I32)
    padded = ((cnt + tm - 1) // tm) * tm
    ends = jnp.cumsum(padded)
    base = ends - padded
    dest = (base[eid] + rank).reshape(-1).astype(I32)
    tile_ids = jnp.arange(n_tiles, dtype=I32)
    tile_expert = jnp.sum((tile_ids[:, None] >= (ends // tm)[None, :]).astype(I32), axis=1)
    tile_expert = jnp.minimum(tile_expert, N_EXPERTS - 1).astype(I32)
    n_used = (ends[-1] // tm).reshape(1).astype(I32)
    return dest, tile_expert, n_used


def kernel(x, c, ctx, c_ctx, ada_w, ada_b, norm1_g, norm2_g, w_in, w_out, conv_a_w, conv_a_b, ln_a_g, ln_a_b, diff_lambda, diff_norm_g, conv_c_w, norm_c_g, lru_conv_w, lru_conv_b, lru_wa, lru_ba, lru_wx, lru_bx, lru_lam, norm_d_g, router_w, router_b, exp_w1, exp_b1, exp_w2, exp_b2, final_g):
    bsz, seq, d = x.shape
    ctx_len = ctx.shape[1]
    depth = ada_w.shape[0]
    n_lat = bsz * seq
    n_ctx = bsz * ctx_len
    t_all = n_lat + n_ctx
    assert bsz + 1 <= MOD_ROWS and seq % ctx_len == 0 and ctx_len % 128 == 0
    tm_in = min(1024, seq, n_ctx)
    assert seq % tm_in == 0 and n_ctx % tm_in == 0
    tr = ctx_len if ctx_len <= 256 else 256
    assert ctx_len % tr == 0
    tm_ffn = 256
    tc = 128
    words = d // 2 // LANES

    cc = jnp.zeros((MOD_ROWS, d), F32).at[:bsz].set(c).at[bsz].set(c_ctx)
    mod = _modulation(cc, ada_w, ada_b)
    cos2, sin2 = _rope_tables(seq)
    xa = jnp.concatenate([x.reshape(n_lat, d), ctx.reshape(n_ctx, d)], axis=0)

    for i in range(depth):
        need_ctx = i < depth - 1
        lam_init = 0.8 - 0.6 * math.exp(-0.3 * i)
        n_rows = t_all if need_ctx else n_lat
        mod3 = mod[i].reshape(MOD_ROWS, 1, 6 * d)

        u = _inproj(xa, mod3, norm1_g[i], w_in[i].astype(BF16),
                    n_lat_rows=n_lat, seq=seq, bsz=bsz, tm=tm_in)
        ya, ys = _convmix(u, conv_a_w[i], conv_a_b[i], ln_a_g[i], ln_a_b[i], conv_c_w[i], norm_c_g[i],
                          n_rows=n_rows, n_lat_rows=n_lat, seq=seq, ctx_len=ctx_len, tr=tr)
        yb = _attention(u, cos2, sin2, diff_lambda[i], diff_norm_g[i], bsz=bsz, seq=seq,
                        ctx_len=ctx_len, need_ctx=need_ctx, lam_init=lam_init)
        yd = _rglru(u, lru_conv_w[i], lru_conv_b[i], _block_diag(lru_wa[i]).astype(BF16), lru_ba[i],
                    _block_diag(lru_wx[i]).astype(BF16), lru_bx[i], lru_lam[i], norm_d_g[i],
                    bsz=bsz, seq=seq, ctx_len=ctx_len, need_ctx=need_ctx)
        xm, hp, eid, gate, rank, counts = _outproj(
            ya, yb, ys, yd, w_out[i].astype(BF16), xa, mod3, norm2_g[i], router_w[i], router_b[i],
            n_rows=n_rows, n_lat_rows=n_lat, seq=seq, bsz=bsz, tm=tr)

        n_tiles = (n_rows * TOP_K) // tm_ffn + N_EXPERTS
        dest, tile_expert, n_used = _moe_plan(eid, rank, counts, tm=tm_ffn, n_tiles=n_tiles)
        xs = _dispatch(dest, hp, jnp.zeros((n_tiles * tm_ffn * words, LANES), U32),
                       n_tok=n_rows, tm=tr, words=words)
        ff = exp_w1.shape[3] // 2
        w1g, w1u = _w1prep(exp_w1, i)
        ysort = _expert_ffn(
            tile_expert, n_used, xs, w1g, w1u,
            exp_b1[i][:, 0::2].reshape(N_EXPERTS, 1, ff), exp_b1[i][:, 1::2].reshape(N_EXPERTS, 1, ff),
            exp_w2[i].astype(BF16), exp_b2[i].reshape(N_EXPERTS, 1, d),
            n_tiles=n_tiles, tm=tm_ffn, d=d)
        xa = _combine(dest, gate, xm, mod3, final_g, ysort, n_tok=n_rows, n_lat_rows=n_lat,
                      seq=seq, bsz=bsz, tc=tc, final=not need_ctx)

    return xa[:n_lat].reshape(bsz, seq, d)
```
